```python
import jax, jax.numpy as jnp
from jax import lax
import numpy as np

D_MODEL = 1024
BATCH = 4
SEQ = 8192
DEPTH = 1

HEAD_DIM = 64
SB_HEADS = 8
DIL_GROUPS = ((128, 1), (512, 4), (2048, 16))
DIL_HEADS_PER_GROUP = 4
N_DIL_GROUPS = len(DIL_GROUPS)
DIL_HEADS = DIL_HEADS_PER_GROUP * N_DIL_GROUPS
SB_WIDTH = SB_HEADS * HEAD_DIM
DIL_WIDTH = DIL_HEADS * HEAD_DIM
DIL_OUT_WIDTH = DIL_HEADS_PER_GROUP * HEAD_DIM
IN_COLS = 3 * SB_WIDTH + 3 * DIL_WIDTH + 2 * D_MODEL
Q_BLOCK = 128
ROPE_THETA = 10000.0
PEER_HEADS = 8
PEER_NKEYS = 128
PEER_EXPERTS = PEER_NKEYS * PEER_NKEYS
PEER_QDIM = 256
PEER_HALF = PEER_QDIM // 2
PEER_TOPK = 16
PEER_TOKEN_CHUNK = 128
N_MOD = 6
EPS = 1e-6

kernel_name = "hybrid_stickbreak_dilated_peer_adaln"


def rms_norm(x, g):
    xf = x.astype(jnp.float32)
    y = xf * lax.rsqrt(jnp.mean(xf * xf, axis=-1, keepdims=True) + EPS)
    return (y * g.astype(jnp.float32)).astype(x.dtype)


def modulate(h, shift, scale):
    return h * (1.0 + scale[:, None, :]) + shift[:, None, :]


def rope(x, pos):
    half = HEAD_DIM // 2
    inv = ROPE_THETA ** (-jnp.arange(half, dtype=jnp.float32) / half)
    ang = pos.astype(jnp.float32)[:, None] * inv[None, :]
    cos = jnp.cos(ang)[None, :, None, :]
    sin = jnp.sin(ang)[None, :, None, :]
    xf = x.astype(jnp.float32)
    x1, x2 = xf[..., :half], xf[..., half:]
    return jnp.concatenate([x1 * cos - x2 * sin, x1 * sin + x2 * cos], axis=-1).astype(x.dtype)


def stick_breaking_attention(q, k, v):
    S = q.shape[1]
    scale = HEAD_DIM ** -0.5
    outs = []
    for blk in range(S // Q_BLOCK):
        t0 = blk * Q_BLOCK
        kl = t0 + Q_BLOCK
        qb, kb, vb = q[:, t0:kl], k[:, :kl], v[:, :kl]
        z = jnp.einsum('bqhd,bkhd->bhqk', qb, kb).astype(jnp.float32) * scale
        t_idx = t0 + jnp.arange(Q_BLOCK)
        s_idx = jnp.arange(kl)
        mask = s_idx[None, :] < t_idx[:, None]
        log_1m_beta = jnp.where(mask, jax.nn.log_sigmoid(-z), 0.0)
        after = lax.cumsum(log_1m_beta, axis=3, reverse=True) - log_1m_beta
        a = jnp.where(mask, jnp.exp(jax.nn.log_sigmoid(z) + after), 0.0)
        outs.append(jnp.einsum('bhqk,bkhd->bqhd', a.astype(v.dtype), vb))
    return jnp.concatenate(outs, axis=1)


def dilated_attention(q, k, v):
    B, S = q.shape[:2]
    scale = HEAD_DIM ** -0.5
    n_blocks = S // Q_BLOCK

    def block(blk):
        t = blk * Q_BLOCK + jnp.arange(Q_BLOCK)
        qb = lax.dynamic_slice_in_dim(q, blk * Q_BLOCK, Q_BLOCK, axis=1)
        outs, lses = [], []
        for g, (w, r) in enumerate(DIL_GROUPS):
            n = w // r + 1
            idx = t[:, None] - r * jnp.arange(n)[None, :]
            valid = idx >= 0
            idx_c = jnp.maximum(idx, 0)
            kg = jnp.take(k[:, :, g], idx_c, axis=1)
            vg = jnp.take(v[:, :, g], idx_c, axis=1)
            z = jnp.einsum('bqhd,bqnhd->bhqn', qb[:, :, g], kg).astype(jnp.float32) * scale
            z = jnp.where(valid[None, None], z, -jnp.inf)
            lse = jax.nn.logsumexp(z, axis=-1)
            p = jnp.exp(z - lse[..., None])
            outs.append(jnp.einsum('bhqn,bqnhd->bqhd', p.astype(vg.dtype), vg).astype(jnp.float32))
            lses.append(lse)
        wts = jax.nn.softmax(jnp.stack(lses, axis=0), axis=0)
        wts = jnp.transpose(wts, (0, 1, 3, 2))[..., None]
        return jnp.sum(wts * jnp.stack(outs, axis=0), axis=0).astype(q.dtype)

    out = lax.map(block, jnp.arange(n_blocks))
    return jnp.transpose(out, (1, 0, 2, 3, 4)).reshape(B, S, DIL_HEADS_PER_GROUP, HEAD_DIM)


def peer_ffn(h, w_pq, peer_keys, peer_u, peer_v):
    B, S, D = h.shape
    T = B * S
    ht = h.reshape(T, D)
    q = (ht @ w_pq).reshape(T, PEER_HEADS, 2, PEER_HALF)
    sub = jnp.einsum('thpc,hpnc->thpn', q, peer_keys).astype(jnp.float32)
    s_top, i_top = lax.top_k(sub, PEER_TOPK)
    cand_s = (s_top[:, :, 0, :, None] + s_top[:, :, 1, None, :]).reshape(T, PEER_HEADS, PEER_TOPK * PEER_TOPK)
    cand_i = (i_top[:, :, 0, :, None] * PEER_NKEYS + i_top[:, :, 1, None, :]).reshape(T, PEER_HEADS, PEER_TOPK * PEER_TOPK)
    best_s, pos = lax.top_k(cand_s, PEER_TOPK)
    expert = jnp.take_along_axis(cand_i, pos, axis=-1)
    gate = jax.nn.softmax(best_s, axis=-1)
    hk = PEER_HEADS * PEER_TOPK
    n_chunks = T // PEER_TOKEN_CHUNK

    def chunk(args):
        xc, ec, gc = args
        u = jnp.take(peer_u, ec, axis=0)
        act = jax.nn.gelu(jnp.einsum('cd,ckd->ck', xc, u))
        vv = jnp.take(peer_v, ec, axis=0)
        return jnp.einsum('ck,ckd->cd', gc.astype(xc.dtype) * act, vv)

    out = lax.map(chunk, (ht.reshape(n_chunks, PEER_TOKEN_CHUNK, D),
                          expert.reshape(n_chunks, PEER_TOKEN_CHUNK, hk),
                          gate.reshape(n_chunks, PEER_TOKEN_CHUNK, hk)))
    return out.reshape(B, S, D)


def setup_inputs(seed: int = 0) -> dict:
    key = jax.random.key(seed)
    ks = jax.random.split(key, 16)
    D = D_MODEL
    f = jnp.float32
    nrm = lambda k, shape, s: jax.random.normal(k, shape, f) * s
    return {
        "x": nrm(ks[0], (BATCH, SEQ, D), 1.0),
        "c": nrm(ks[1], (BATCH, D), 1.0),
        "w_ada": nrm(ks[2], (DEPTH, D, N_MOD * D), 0.5 * D ** -0.5),
        "b_ada": nrm(ks[3], (DEPTH, N_MOD * D), 0.01),
        "g_mix": 1.0 + nrm(ks[4], (DEPTH, D), 0.02),
        "w_in": nrm(ks[5], (DEPTH, D, IN_COLS), D ** -0.5),
        "w_sb_o": nrm(ks[6], (DEPTH, SB_WIDTH, D), SB_WIDTH ** -0.5),
        "w_dil_o": nrm(ks[7], (DEPTH, DIL_OUT_WIDTH, D), DIL_OUT_WIDTH ** -0.5),
        "w_out": nrm(ks[8], (DEPTH, D, D), D ** -0.5),
        "g_ffn": 1.0 + nrm(ks[9], (DEPTH, D), 0.02),
        "w_pq": nrm(ks[10], (DEPTH, D, PEER_HEADS * PEER_QDIM), D ** -0.5),
        "peer_keys": nrm(ks[11], (DEPTH, PEER_HEADS, 2, PEER_NKEYS, PEER_HALF), PEER_HALF ** -0.5),
        "peer_u": nrm(ks[12], (DEPTH, PEER_EXPERTS, D), D ** -0.5),
        "peer_v": nrm(ks[13], (DEPTH, PEER_EXPERTS, D), PEER_HEADS ** -0.5),
        "g_final": 1.0 + nrm(ks[14], (D,), 0.02),
    }


def reference(x, c, w_ada, b_ada, g_mix, w_in, w_sb_o, w_dil_o, w_out, g_ffn, w_pq,
              peer_keys, peer_u, peer_v, g_final):
    B, S, D = x.shape
    pos = jnp.arange(S)
    splits = np.cumsum([SB_WIDTH] * 3 + [DIL_WIDTH] * 3 + [D_MODEL]).tolist()
    for layer in range(DEPTH):
        mod = jax.nn.silu(c) @ w_ada[layer] + b_ada[layer]
        sh1, sc1, gt1, sh2, sc2, gt2 = jnp.split(mod, N_MOD, axis=-1)

        h = modulate(rms_norm(x, g_mix[layer]), sh1, sc1)
        proj = h @ w_in[layer]
        q_sb, k_sb, v_sb, q_d, k_d, v_d, gate_sb, gate_d = jnp.split(proj, splits, axis=-1)

        hs = (B, S, SB_HEADS, HEAD_DIM)
        o_sb = stick_breaking_attention(q_sb.reshape(hs), k_sb.reshape(hs), v_sb.reshape(hs))

        hd = (B, S, DIL_HEADS, HEAD_DIM)
        gd = (B, S, N_DIL_GROUPS, DIL_HEADS_PER_GROUP, HEAD_DIM)
        q_dr = rope(q_d.reshape(hd), pos).reshape(gd)
        k_dr = rope(k_d.reshape(hd), pos).reshape(gd)
        o_d = dilated_attention(q_dr, k_dr, v_d.reshape(gd))

        br_sb = o_sb.reshape(B, S, SB_WIDTH) @ w_sb_o[layer]
        br_d = o_d.reshape(B, S, DIL_OUT_WIDTH) @ w_dil_o[layer]
        merged = jax.nn.sigmoid(gate_sb) * br_sb + jax.nn.sigmoid(gate_d) * br_d
        x = x + gt1[:, None, :] * (merged @ w_out[layer])

        h2 = modulate(rms_norm(x, g_ffn[layer]), sh2, sc2)
        x = x + gt2[:, None, :] * peer_ffn(h2, w_pq[layer], peer_keys[layer], peer_u[layer], peer_v[layer])
    return rms_norm(x, g_final)
```

```python
import functools

import jax
import jax.numpy as jnp
from jax import lax
from jax.experimental import pallas as pl
from jax.experimental.pallas import tpu as pltpu

F32 = jnp.float32
BF16 = jnp.bfloat16

HEAD_DIM = 64
SB_HEADS = 8
DIL_GROUPS = ((128, 1), (512, 4), (2048, 16))
DIL_HEADS_PER_GROUP = 4
SB_WIDTH = SB_HEADS * HEAD_DIM
DIL_WIDTH = len(DIL_GROUPS) * DIL_HEADS_PER_GROUP * HEAD_DIM
DIL_OUT_WIDTH = DIL_HEADS_PER_GROUP * HEAD_DIM
ROPE_THETA = 10000.0
PEER_HEADS = 8
PEER_NKEYS = 128
PEER_HALF = 128
PEER_TOPK = 16
N_MOD = 6
EPS = 1e-6

LANES = 128
SUBLANES = 8
VMEM_LIMIT = 56 * 1024 * 1024
NEG_BIG = -1e30


def _cparams(sem):
    return pltpu.CompilerParams(dimension_semantics=sem, vmem_limit_bytes=VMEM_LIMIT)


def _const_spec(shape):
    nd = len(shape)
    return pl.BlockSpec(shape, lambda *_: (0,) * nd, pipeline_mode=pl.Buffered(1))


def _adaln_kernel(c_ref, w_ref, b_ref, o_ref):
    c = c_ref[...]
    s = c * jax.nn.sigmoid(c)
    o_ref[...] = jnp.dot(s, w_ref[...], preferred_element_type=F32) + b_ref[...]


def _adaln(c, w, b):
    bsz, d = c.shape
    n = w.shape[1]
    bn = 1536
    return pl.pallas_call(
        _adaln_kernel,
        grid=(n // bn,),
        in_specs=[pl.BlockSpec((bsz, d), lambda j: (0, 0)),
                  pl.BlockSpec((d, bn), lambda j: (0, j)),
                  pl.BlockSpec((1, bn), lambda j: (0, j))],
        out_specs=pl.BlockSpec((bsz, bn), lambda j: (0, j)),
        out_shape=jax.ShapeDtypeStruct((bsz, n), F32),
        compiler_params=_cparams(("arbitrary",)),
        name="adaln",
    )(c, w, b.reshape(1, n))


def _rms_mod(x, g, shift, scale):
    ms = jnp.mean(x * x, axis=-1, keepdims=True)
    y = x * lax.rsqrt(ms + EPS) * g
    return y * (1.0 + scale) + shift


def _rope(x, cos, sin):
    lane = lax.broadcasted_iota(jnp.int32, x.shape, 1)
    first_half = (lane & (HEAD_DIM - 1)) < (HEAD_DIM // 2)
    swapped = jnp.where(first_half,
                        pltpu.roll(x, LANES - HEAD_DIM // 2, axis=1),
                        pltpu.roll(x, HEAD_DIM // 2, axis=1))
    return x * cos + swapped * sin


def _inproj_kernel(x_ref, mod_ref, g_ref, w_ref, cos_ref, sin_ref,
                   qs_ref, ks_ref, vs_ref, qd_ref, kd_ref, vd_ref, gate_ref):
    x = x_ref[0]
    h = _rms_mod(x, g_ref[...], mod_ref[0, 0:1, :], mod_ref[0, 1:2, :]).astype(BF16)
    scale = HEAD_DIM ** -0.5

    def proj(lo, n):
        return jnp.dot(h, w_ref[:, lo:lo + n], preferred_element_type=F32)

    o = 0
    qs_ref[0] = (proj(o, SB_WIDTH) * scale).astype(BF16); o += SB_WIDTH
    ks_ref[0] = proj(o, SB_WIDTH).astype(BF16); o += SB_WIDTH
    vs_ref[0] = proj(o, SB_WIDTH).astype(BF16); o += SB_WIDTH
    cos = cos_ref[...]
    sin = sin_ref[...]
    for ref, sc in ((qd_ref, scale), (kd_ref, 1.0)):
        for j in range(DIL_WIDTH // LANES):
            p = proj(o + j * LANES, LANES)
            ref[0, :, j * LANES:(j + 1) * LANES] = (_rope(p, cos, sin) * sc).astype(BF16)
        o += DIL_WIDTH
    vd_ref[0] = proj(o, DIL_WIDTH).astype(BF16); o += DIL_WIDTH
    d = x.shape[-1]
    gate_ref[0] = jax.nn.sigmoid(proj(o, 2 * d))


def _inproj(x, mod, g_mix, w_in_bf, cos, sin, tm=256):
    b, s, d = x.shape
    ncols = w_in_bf.shape[1]
    tok = lambda w: pl.BlockSpec((1, tm, w), lambda bi, i: (bi, i, 0))
    outs = [(SB_WIDTH, BF16)] * 3 + [(DIL_WIDTH, BF16)] * 3 + [(2 * d, F32)]
    return pl.pallas_call(
        _inproj_kernel,
        grid=(b, s // tm),
        in_specs=[tok(d),
                  pl.BlockSpec((1, N_MOD, d), lambda bi, i: (bi, 0, 0)),
                  _const_spec((1, d)),
                  _const_spec((d, ncols)),
                  pl.BlockSpec((tm, LANES), lambda bi, i: (i, 0)),
                  pl.BlockSpec((tm, LANES), lambda bi, i: (i, 0))],
        out_specs=[tok(w) for w, _ in outs],
        out_shape=[jax.ShapeDtypeStruct((b, s, w), dt) for w, dt in outs],
        compiler_params=_cparams(("parallel", "parallel")),
        name="inproj",
    )(x, mod, g_mix, w_in_bf, cos, sin)


SB_BLOCK = 256


def _softplus(z):
    return jnp.maximum(z, 0.0) + jnp.log1p(jnp.exp(-jnp.abs(z)))


def _sb_kernel(q_ref, k_ref, v_ref, tri_ref, o_ref):
    i = pl.program_id(2)
    blk = SB_BLOCK
    q = q_ref[0]
    lane = lax.broadcasted_iota(jnp.int32, q.shape, 1)
    qh = (jnp.where(lane < HEAD_DIM, q, jnp.zeros_like(q)),
          jnp.where(lane >= HEAD_DIM, q, jnp.zeros_like(q)))
    tri = tri_ref[...]
    row = lax.broadcasted_iota(jnp.int32, (blk, blk), 0)
    col = lax.broadcasted_iota(jnp.int32, (blk, blk), 1)
    dcr = col - row

    def body(j, carry):
        start = pl.multiple_of((i - j) * blk, blk)
        k = k_ref[0, pl.ds(start, blk), :]
        v = v_ref[0, pl.ds(start, blk), :]
        valid = dcr < j * blk
        new = []
        for h in range(2):
            run, acc = carry[2 * h], carry[2 * h + 1]
            z = lax.dot_general(qh[h], k, (((1,), (1,)), ((), ())), preferred_element_type=F32)
            lsm = jnp.where(valid, -_softplus(z), 0.0)
            hi = lsm.astype(BF16)
            lo = (lsm - hi.astype(F32)).astype(BF16)
            cs = (jnp.dot(hi, tri, preferred_element_type=F32)
                  + jnp.dot(lo, tri, preferred_element_type=F32))
            a = jnp.where(valid, jnp.exp(z + cs + run), 0.0)
            acc = acc + jnp.dot(a.astype(BF16), v, preferred_element_type=F32)
            new += [run + cs[:, 0:1], acc]
        return tuple(new)

    zero_run = jnp.zeros((blk, 1), F32)
    zero_acc = jnp.zeros((blk, LANES), F32)
    res = lax.fori_loop(0, i + 1, body, (zero_run, zero_acc, zero_run, zero_acc))
    o_ref[0] = jnp.where(lane < HEAD_DIM, res[1], res[3]).astype(o_ref.dtype)


def _sb_attention(q, k, v):
    b, s, w = q.shape
    blk = SB_BLOCK
    r = lax.broadcasted_iota(jnp.int32, (blk, blk), 0)
    c = lax.broadcasted_iota(jnp.int32, (blk, blk), 1)
    tri = (r >= c).astype(BF16)
    return pl.pallas_call(
        _sb_kernel,
        grid=(b, w // LANES, s // blk),
        in_specs=[pl.BlockSpec((1, blk, LANES), lambda bi, p, i: (bi, i, p)),
                  pl.BlockSpec((1, s, LANES), lambda bi, p, i: (bi, 0, p)),
                  pl.BlockSpec((1, s, LANES), lambda bi, p, i: (bi, 0, p)),
                  _const_spec((blk, blk))],
        out_specs=pl.BlockSpec((1, blk, LANES), lambda bi, p, i: (bi, i, p)),
        out_shape=jax.ShapeDtypeStruct((b, s, w), BF16),
        compiler_params=_cparams(("parallel", "parallel", "arbitrary")),
        name="sb_attn",
    )(q, k, v, tri)


DIL_BLOCK = 128


def _dil_kernel(q_ref, kp_ref, kc_ref, vp_ref, vc_ref, o_ref, lse_ref):
    i = pl.program_id(2)
    blk = DIL_BLOCK
    q = q_ref[0]
    k2 = jnp.concatenate([kp_ref[0], kc_ref[0]], axis=0)
    v2 = jnp.concatenate([vp_ref[0], vc_ref[0]], axis=0)
    row = lax.broadcasted_iota(jnp.int32, (blk, 2 * blk), 0)
    col = lax.broadcasted_iota(jnp.int32, (blk, 2 * blk), 1)
    valid = (col >= row) & (col <= row + blk) & ((col >= blk) | (i > 0))
    lane = lax.broadcasted_iota(jnp.int32, q.shape, 1)
    out = jnp.zeros(q.shape, F32)
    lse_full = jnp.zeros(q.shape, F32)
    for h in range(DIL_HEADS_PER_GROUP):
        mine = (lane >= h * HEAD_DIM) & (lane < (h + 1) * HEAD_DIM)
        qh = jnp.where(mine, q, jnp.zeros_like(q))
        z = lax.dot_general(qh, k2, (((1,), (1,)), ((), ())), preferred_element_type=F32)
        z = jnp.where(valid, z, NEG_BIG)
        m = jnp.max(z, axis=-1, keepdims=True)
        p = jnp.exp(z - m)
        l = jnp.sum(p, axis=-1, keepdims=True)
        oh = jnp.dot(p.astype(BF16), v2, preferred_element_type=F32) / l
        out = jnp.where(mine, oh, out)
        lse_full = jnp.where(mine, m + jnp.log(l), lse_full)
    o_ref[0] = out
    lse_ref[0] = lse_full


def _dil_attention_group(q, k, v, g, r):
    b, s, w = q.shape
    sr = s // r
    blk = DIL_BLOCK
    gw = DIL_OUT_WIDTH
    per = w // gw
    qv, kv, vv = (a.reshape(b, sr, r * w) for a in (q, k, v))
    cur = pl.BlockSpec((1, blk, gw), lambda bi, c, i: (bi, i, per * c + g))
    prev = pl.BlockSpec((1, blk, gw), lambda bi, c, i: (bi, jnp.maximum(i - 1, 0), per * c + g))
    ospec = pl.BlockSpec((1, blk, gw), lambda bi, c, i: (bi, i, c))
    o, lse = pl.pallas_call(
        _dil_kernel,
        grid=(b, r, sr // blk),
        in_specs=[cur, prev, cur, prev, cur],
        out_specs=[ospec, ospec],
        out_shape=[jax.ShapeDtypeStruct((b, sr, r * gw), F32)] * 2,
        compiler_params=_cparams(("parallel", "parallel", "arbitrary")),
        name=f"dil_attn_r{r}",
    )(qv, kv, kv, vv, vv)
    return o.reshape(b, s, gw), lse.reshape(b, s, gw)


def _merge_kernel(osb_ref, o0_ref, o1_ref, o2_ref, l0_ref, l1_ref, l2_ref, gate_ref, x_ref, mod_ref,
                  wsb_ref, wdil_ref, wout_ref, gffn_ref, wpq_ref, keys_ref,
                  x1_ref, h2_ref, sub_ref):
    d = x_ref.shape[-1]
    l0, l1, l2 = l0_ref[0], l1_ref[0], l2_ref[0]
    m = jnp.maximum(jnp.maximum(l0, l1), l2)
    e0, e1, e2 = jnp.exp(l0 - m), jnp.exp(l1 - m), jnp.exp(l2 - m)
    o_d = (e0 * o0_ref[0] + e1 * o1_ref[0] + e2 * o2_ref[0]) / (e0 + e1 + e2)
    br_sb = jnp.dot(osb_ref[0], wsb_ref[...], preferred_element_type=F32)
    br_d = jnp.dot(o_d.astype(BF16), wdil_ref[...], preferred_element_type=F32)
    merged = gate_ref[0, :, 0:d] * br_sb + gate_ref[0, :, d:2 * d] * br_d
    x1 = x_ref[0] + mod_ref[0, 2:3, :] * jnp.dot(merged.astype(BF16), wout_ref[...],
                                                 preferred_element_type=F32)
    x1_ref[0] = x1
    h2 = _rms_mod(x1, gffn_ref[...], mod_ref[0, 3:4, :], mod_ref[0, 4:5, :])
    h2_ref[0] = h2
    pq = jnp.dot(h2.astype(BF16), wpq_ref[...], preferred_element_type=F32).astype(BF16)
    for hp in range(2 * PEER_HEADS):
        sub_ref[hp] = lax.dot_general(keys_ref[hp], pq[:, hp * PEER_HALF:(hp + 1) * PEER_HALF],
                                      (((1,), (1,)), ((), ())), preferred_element_type=F32)


def _merge(o_sb, o_g, lse_g, gates, x, mod, w_sb_o, w_dil_o, w_out, g_ffn, w_pq, keys, tm=256):
    b, s, d = x.shape
    nt = s // tm
    tok = lambda w: pl.BlockSpec((1, tm, w), lambda bi, i: (bi, i, 0))
    nhp = 2 * PEER_HEADS
    return pl.pallas_call(
        _merge_kernel,
        grid=(b, nt),
        in_specs=[tok(SB_WIDTH)] + [tok(DIL_OUT_WIDTH)] * 6 + [tok(2 * d), tok(d),
                  pl.BlockSpec((1, N_MOD, d), lambda bi, i: (bi, 0, 0)),
                  _const_spec(w_sb_o.shape), _const_spec(w_dil_o.shape), _const_spec(w_out.shape),
                  _const_spec((1, d)), _const_spec(w_pq.shape), _const_spec(keys.shape)],
        out_specs=[tok(d), tok(d),
                   pl.BlockSpec((nhp, PEER_NKEYS, tm), lambda bi, i: (0, 0, bi * nt + i))],
        out_shape=[jax.ShapeDtypeStruct((b, s, d), F32), jax.ShapeDtypeStruct((b, s, d), F32),
                   jax.ShapeDtypeStruct((nhp, PEER_NKEYS, b * s), F32)],
        compiler_params=_cparams(("parallel", "parallel")),
        name="merge",
    )(o_sb, *o_g, *lse_g, gates, x, mod, w_sb_o, w_dil_o, w_out, g_ffn, w_pq, keys)


def _top_rows(s, k, payload=None):
    n = s.shape[0]
    rowid = lax.broadcasted_iota(jnp.int32, s.shape, 0).astype(F32)
    vals, sel = [], []
    for _ in range(k):
        m = jnp.max(s, axis=0, keepdims=True)
        ix = jnp.min(jnp.where(s == m, rowid, float(n)), axis=0, keepdims=True)
        hit = rowid == ix
        vals.append(m)
        sel.append(ix if payload is None else jnp.max(jnp.where(hit, payload, -1.0), axis=0, keepdims=True))
        s = jnp.where(hit, -jnp.inf, s)
    return jnp.concatenate(vals, axis=0), jnp.concatenate(sel, axis=0)


def _topk_kernel(sub_ref, ids_ref, gate_ref):
    k = PEER_TOPK
    tl = sub_ref.shape[-1]
    jrow = lax.broadcasted_iota(jnp.int32, (k, tl), 0)
    for h in range(PEER_HEADS):
        s0, i0 = _top_rows(sub_ref[2 * h], k)
        s1, i1 = _top_rows(sub_ref[2 * h + 1], k)
        cand, cand_e = [], []
        for i in range(k):
            keep = jrow < (k // (i + 1))
            cand.append(jnp.where(keep, s0[i:i + 1, :] + s1, -jnp.inf))
            cand_e.append(i0[i:i + 1, :] * PEER_NKEYS + i1)
        best, expert = _top_rows(jnp.concatenate(cand, axis=0), k, jnp.concatenate(cand_e, axis=0))
        e = jnp.exp(best - best[0:1, :])
        ids_ref[h * k:(h + 1) * k, :] = expert.astype(jnp.int32)
        gate_ref[h * k:(h + 1) * k, :] = e / jnp.sum(e, axis=0, keepdims=True)


def _peer_topk(sub_t, tl=256):
    nhp, nk, t = sub_t.shape
    hk = PEER_HEADS * PEER_TOPK
    return pl.pallas_call(
        _topk_kernel,
        grid=(t // tl,),
        in_specs=[pl.BlockSpec((nhp, nk, tl), lambda i: (0, 0, i))],
        out_specs=[pl.BlockSpec((hk, tl), lambda i: (0, i))] * 2,
        out_shape=[jax.ShapeDtypeStruct((hk, t), jnp.int32), jax.ShapeDtypeStruct((hk, t), F32)],
        compiler_params=_cparams(("parallel",)),
        name="peer_topk",
    )(sub_t)


PEER_TT = 8
ROW_TILES = 8
ISSUE_UNROLL = 16


def _peer_kernel(ids_ref, gate_ref, x_ref, gsum_ref, gexp_ref, uv_ref, o_ref, buf, sem):
    hk = gate_ref.shape[-1]
    rows = hk * ROW_TILES

    per_tok = hk // ISSUE_UNROLL

    def issue(j, _):
        t = j // per_tok
        k0 = (j - t * per_tok) * ISSUE_UNROLL
        for kk in range(ISSUE_UNROLL):
            e = ids_ref[t, k0 + kk]
            dst = pl.multiple_of(t * rows + (k0 + kk) * ROW_TILES, ROW_TILES)
            pltpu.make_async_copy(uv_ref.at[e], buf.at[:, pl.ds(dst, ROW_TILES), :], sem).start()
        return 0

    lax.fori_loop(0, PEER_TT * hk // ISSUE_UNROLL, issue, 0)
    pltpu.make_async_copy(buf, buf, sem).wait()

    lane = lax.broadcasted_iota(jnp.int32, (ROW_TILES, rows), 1)
    sub = lax.broadcasted_iota(jnp.int32, (ROW_TILES, rows), 0)
    diag = (lane % ROW_TILES) == sub
    zs = []
    for t in range(PEER_TT):
        xs = x_ref[t].astype(BF16)
        ucat = buf[0, t * rows:(t + 1) * rows, :].astype(BF16)
        y = lax.dot_general(xs, ucat, (((1,), (1,)), ((), ())), preferred_element_type=F32)
        zs.append(jnp.sum(jnp.where(diag, y, 0.0), axis=0, keepdims=True))
    z = jnp.concatenate(zs, axis=0)
    zh = z.astype(BF16)
    zl = (z - zh.astype(F32)).astype(BF16)
    act = (jnp.dot(zh, gsum_ref[...], preferred_element_type=F32)
           + jnp.dot(zl, gsum_ref[...], preferred_element_type=F32))
    c = (gate_ref[...] * jax.nn.gelu(act)).astype(BF16)
    crep = jnp.dot(c, gexp_ref[...], preferred_element_type=F32)
    for t in range(PEER_TT):
        cm = jnp.where(diag, jnp.broadcast_to(crep[t:t + 1, :], diag.shape), 0.0).astype(BF16)
        vcat = buf[1, t * rows:(t + 1) * rows, :].astype(BF16)
        o_ref[t] = jnp.dot(cm, vcat, preferred_element_type=F32)


def _peer_ffn(ids, gate, h2_tiles, uv):
    t, hk = ids.shape
    rows = hk * ROW_TILES
    r = lax.broadcasted_iota(jnp.int32, (rows, hk), 0)
    c = lax.broadcasted_iota(jnp.int32, (rows, hk), 1)
    gsum = (r // ROW_TILES == c).astype(BF16)
    gexp = gsum.T
    tt = PEER_TT
    return pl.pallas_call(
        _peer_kernel,
        grid=(t // tt,),
        in_specs=[pl.BlockSpec((tt, hk), lambda i: (i, 0), memory_space=pltpu.SMEM),
                  pl.BlockSpec((tt, hk), lambda i: (i, 0)),
                  pl.BlockSpec((tt, ROW_TILES, LANES), lambda i: (i, 0, 0)),
                  _const_spec((rows, hk)), _const_spec((hk, rows)),
                  pl.BlockSpec(memory_space=pl.ANY)],
        out_specs=pl.BlockSpec((tt, ROW_TILES, LANES), lambda i: (i, 0, 0)),
        out_shape=jax.ShapeDtypeStruct((t, ROW_TILES, LANES), F32),
        scratch_shapes=[pltpu.VMEM((2, tt * rows, LANES), F32), pltpu.SemaphoreType.DMA(())],
        compiler_params=_cparams(("arbitrary",)),
        name="peer_ffn",
    )(ids, gate, h2_tiles, gsum, gexp, uv)


def _final_kernel(x1_ref, f_ref, mod_ref, g_ref, o_ref):
    x2 = x1_ref[0] + mod_ref[0, 5:6, :] * f_ref[0]
    ms = jnp.mean(x2 * x2, axis=-1, keepdims=True)
    o_ref[0] = x2 * lax.rsqrt(ms + EPS) * g_ref[...]


def _final(x1, ffn, mod, g_final, tm=512):
    b, s, d = x1.shape
    tok = pl.BlockSpec((1, tm, d), lambda bi, i: (bi, i, 0))
    return pl.pallas_call(
        _final_kernel,
        grid=(b, s // tm),
        in_specs=[tok, tok, pl.BlockSpec((1, N_MOD, d), lambda bi, i: (bi, 0, 0)), _const_spec((1, d))],
        out_specs=tok,
        out_shape=jax.ShapeDtypeStruct((b, s, d), F32),
        compiler_params=_cparams(("parallel", "parallel")),
        name="final",
    )(x1, ffn, mod, g_final)


def _rope_tables(s):
    half = HEAD_DIM // 2
    inv = ROPE_THETA ** (-jnp.arange(half, dtype=F32) / half)
    ang = jnp.arange(s, dtype=F32)[:, None] * inv[None, :]
    cos, sin = jnp.cos(ang), jnp.sin(ang)
    reps = LANES // HEAD_DIM
    return (jnp.tile(jnp.concatenate([cos, cos], axis=1), (1, reps)),
            jnp.tile(jnp.concatenate([-sin, sin], axis=1), (1, reps)))


def kernel(x, c, w_ada, b_ada, g_mix, w_in, w_sb_o, w_dil_o, w_out, g_ffn, w_pq, peer_keys, peer_u, peer_v,
           g_final):
    b, s, d = x.shape
    assert w_ada.shape[0] == 1, "the final residual and norm are fused: single layer only"
    layer = 0
    cos, sin = _rope_tables(s)
    mod = _adaln(c, w_ada[layer], b_ada[layer]).reshape(b, N_MOD, d)
    q_sb, k_sb, v_sb, q_d, k_d, v_d, gates = _inproj(
        x, mod, g_mix[layer].reshape(1, d), w_in[layer].astype(BF16), cos, sin)
    o_sb = _sb_attention(q_sb, k_sb, v_sb)
    o_g, lse_g = zip(*[_dil_attention_group(q_d, k_d, v_d, g, r) for g, (_, r) in enumerate(DIL_GROUPS)])
    keys = peer_keys[layer].reshape(2 * PEER_HEADS, PEER_NKEYS, PEER_HALF).astype(BF16)
    x1, h2, sub_t = _merge(o_sb, o_g, lse_g, gates, x, mod,
                           w_sb_o[layer].astype(BF16), w_dil_o[layer].astype(BF16),
                           w_out[layer].astype(BF16), g_ffn[layer].reshape(1, d),
                           w_pq[layer].astype(BF16), keys)
    ids_t, gate_t = _peer_topk(sub_t)
    n_exp = peer_u.shape[1]
    uv = jnp.stack([peer_u[layer].reshape(n_exp, ROW_TILES, LANES),
                    peer_v[layer].reshape(n_exp, ROW_TILES, LANES)], axis=1)
    ffn = _peer_ffn(ids_t.T, gate_t.T, h2.reshape(b * s, ROW_TILES, LANES), uv)
    return _final(x1, ffn.reshape(b, s, d), mod, g_final.reshape(1, d))
```

```python
import functools

import jax
import jax.numpy as jnp
from jax import lax
from jax.experimental import pallas as pl
from jax.experimental.pallas import tpu as pltpu

F32 = jnp.float32
BF16 = jnp.bfloat16

HEAD_DIM = 64
SB_HEADS = 8
DIL_GROUPS = ((128, 1), (512, 4), (2048, 16))
DIL_HEADS_PER_GROUP = 4
SB_WIDTH = SB_HEADS * HEAD_DIM
DIL_WIDTH = len(DIL_GROUPS) * DIL_HEADS_PER_GROUP * HEAD_DIM
DIL_OUT_WIDTH = DIL_HEADS_PER_GROUP * HEAD_DIM
ROPE_THETA = 10000.0
PEER_HEADS = 8
PEER_NKEYS = 128
PEER_HALF = 128
PEER_TOPK = 16
N_MOD = 6
EPS = 1e-6

LANES = 128
SUBLANES = 8
VMEM_LIMIT = 56 * 1024 * 1024
NEG_BIG = -1e30


def _cparams(sem):
    return pltpu.CompilerParams(dimension_semantics=sem, vmem_limit_bytes=VMEM_LIMIT)


def _const_spec(shape):
    nd = len(shape)
    return pl.BlockSpec(shape, lambda *_: (0,) * nd, pipeline_mode=pl.Buffered(1))


def _adaln_kernel(c_ref, w_ref, b_ref, o_ref):
    c = c_ref[...]
    s = c * jax.nn.sigmoid(c)
    o_ref[...] = jnp.dot(s, w_ref[...], preferred_element_type=F32) + b_ref[...]


def _adaln(c, w, b):
    bsz, d = c.shape
    n = w.shape[1]
    bn = 1536
    return pl.pallas_call(
        _adaln_kernel,
        grid=(n // bn,),
        in_specs=[pl.BlockSpec((bsz, d), lambda j: (0, 0)),
                  pl.BlockSpec((d, bn), lambda j: (0, j)),
                  pl.BlockSpec((1, bn), lambda j: (0, j))],
        out_specs=pl.BlockSpec((bsz, bn), lambda j: (0, j)),
        out_shape=jax.ShapeDtypeStruct((bsz, n), F32),
        compiler_params=_cparams(("arbitrary",)),
        name="adaln",
    )(c, w, b.reshape(1, n))


def _rms_mod(x, g, shift, scale):
    ms = jnp.mean(x * x, axis=-1, keepdims=True)
    y = x * lax.rsqrt(ms + EPS) * g
    return y * (1.0 + scale) + shift


def _rope(x, cos, sin):
    lane = lax.broadcasted_iota(jnp.int32, x.shape, 1)
    first_half = (lane & (HEAD_DIM - 1)) < (HEAD_DIM // 2)
    swapped = jnp.where(first_half,
                        pltpu.roll(x, LANES - HEAD_DIM // 2, axis=1),
                        pltpu.roll(x, HEAD_DIM // 2, axis=1))
    return x * cos + swapped * sin


def _inproj_kernel(x_ref, mod_ref, g_ref, w_ref, cos_ref, sin_ref,
                   qs_ref, ks_ref, vs_ref, qd_ref, kd_ref, vd_ref, gate_ref):
    x = x_ref[0]
    h = _rms_mod(x, g_ref[...], mod_ref[0, 0:1, :], mod_ref[0, 1:2, :]).astype(BF16)
    scale = HEAD_DIM ** -0.5

    def proj(lo, n):
        return jnp.dot(h, w_ref[:, lo:lo + n], preferred_element_type=F32)

    o = 0
    qs_ref[0] = (proj(o, SB_WIDTH) * scale).astype(BF16); o += SB_WIDTH
    ks_ref[0] = proj(o, SB_WIDTH).astype(BF16); o += SB_WIDTH
    vs_ref[0] = proj(o, SB_WIDTH).astype(BF16); o += SB_WIDTH
    cos = cos_ref[...]
    sin = sin_ref[...]
    for ref, sc in ((qd_ref, scale), (kd_ref, 1.0)):
        for j in range(DIL_WIDTH // LANES):
            p = proj(o + j * LANES, LANES)
            ref[0, :, j * LANES:(j + 1) * LANES] = (_rope(p, cos, sin) * sc).astype(BF16)
        o += DIL_WIDTH
    vd_ref[0] = proj(o, DIL_WIDTH).astype(BF16); o += DIL_WIDTH
    d = x.shape[-1]
    gate_ref[0] = jax.nn.sigmoid(proj(o, 2 * d))


def _inproj(x, mod, g_mix, w_in_bf, cos, sin, tm=256):
    b, s, d = x.shape
    ncols = w_in_bf.shape[1]
    tok = lambda w: pl.BlockSpec((1, tm, w), lambda bi, i: (bi, i, 0))
    outs = [(SB_WIDTH, BF16)] * 3 + [(DIL_WIDTH, BF16)] * 3 + [(2 * d, F32)]
    return pl.pallas_call(
        _inproj_kernel,
        grid=(b, s // tm),
        in_specs=[tok(d),
                  pl.BlockSpec((1, N_MOD, d), lambda bi, i: (bi, 0, 0)),
                  _const_spec((1, d)),
                  _const_spec((d, ncols)),
                  pl.BlockSpec((tm, LANES), lambda bi, i: (i, 0)),
                  pl.BlockSpec((tm, LANES), lambda bi, i: (i, 0))],
        out_specs=[tok(w) for w, _ in outs],
        out_shape=[jax.ShapeDtypeStruct((b, s, w), dt) for w, dt in outs],
        compiler_params=_cparams(("parallel", "parallel")),
        name="inproj",
    )(x, mod, g_mix, w_in_bf, cos, sin)


SB_BLOCK = 256
F32_EXP_UNDERFLOW = -104.0


def _neg_softplus(z):
    return -(jnp.maximum(z, 0.0) + jnp.log(1.0 + jnp.exp(-jnp.abs(z))))


def _sb_kernel(q_ref, k_ref, v_ref, tri_ref, o_ref):
    i = pl.program_id(2)
    blk = SB_BLOCK
    q = q_ref[0]
    lane = lax.broadcasted_iota(jnp.int32, q.shape, 1)
    qh = (jnp.where(lane < HEAD_DIM, q, jnp.zeros_like(q)),
          jnp.where(lane >= HEAD_DIM, q, jnp.zeros_like(q)))
    tri = tri_ref[...]

    def block(kb, carry, causal):
        start = pl.multiple_of(kb * blk, blk)
        k = k_ref[0, pl.ds(start, blk), :]
        v = v_ref[0, pl.ds(start, blk), :]
        if causal:
            row = lax.broadcasted_iota(jnp.int32, (blk, blk), 0)
            col = lax.broadcasted_iota(jnp.int32, (blk, blk), 1)
            valid = col < row
        new = []
        for h in range(2):
            run, acc = carry[2 * h], carry[2 * h + 1]
            z = lax.dot_general(qh[h], k, (((1,), (1,)), ((), ())), preferred_element_type=F32)
            lsm = _neg_softplus(z)
            if causal:
                lsm = jnp.where(valid, lsm, 0.0)
            hi = lsm.astype(BF16)
            lo = (lsm - hi.astype(F32)).astype(BF16)
            cs = (jnp.dot(hi, tri, preferred_element_type=F32)
                  + jnp.dot(lo, tri, preferred_element_type=F32))
            a = jnp.exp(z + cs + run)
            if causal:
                a = jnp.where(valid, a, 0.0)
            acc = acc + jnp.dot(a.astype(BF16), v, preferred_element_type=F32)
            new += [run + cs[:, 0:1], acc]
        return tuple(new)

    def alive(carry):
        top = jnp.max(jnp.maximum(carry[0], carry[2]), axis=0, keepdims=True)
        return (top[0, 0] >= F32_EXP_UNDERFLOW).astype(jnp.int32)

    zero_run = jnp.zeros((blk, 1), F32)
    zero_acc = jnp.zeros((blk, LANES), F32)
    first = block(i, (zero_run, zero_acc, zero_run, zero_acc), causal=True)

    def cond(state):
        return (state[0] >= 0) & (state[1] > 0)

    def body(state):
        kb, _, carry = state
        carry = block(kb, carry, causal=False)
        return kb - 1, alive(carry), carry

    res = lax.while_loop(cond, body, (i - 1, alive(first), first))[2]
    o_ref[0] = jnp.where(lane < HEAD_DIM, res[1], res[3]).astype(o_ref.dtype)


def _sb_attention(q, k, v):
    b, s, w = q.shape
    blk = SB_BLOCK
    r = lax.broadcasted_iota(jnp.int32, (blk, blk), 0)
    c = lax.broadcasted_iota(jnp.int32, (blk, blk), 1)
    tri = (r >= c).astype(BF16)
    return pl.pallas_call(
        _sb_kernel,
        grid=(b, w // LANES, s // blk),
        in_specs=[pl.BlockSpec((1, blk, LANES), lambda bi, p, i: (bi, i, p)),
                  pl.BlockSpec((1, s, LANES), lambda bi, p, i: (bi, 0, p)),
                  pl.BlockSpec((1, s, LANES), lambda bi, p, i: (bi, 0, p)),
                  _const_spec((blk, blk))],
        out_specs=pl.BlockSpec((1, blk, LANES), lambda bi, p, i: (bi, i, p)),
        out_shape=jax.ShapeDtypeStruct((b, s, w), BF16),
        compiler_params=_cparams(("parallel", "parallel", "arbitrary")),
        name="sb_attn",
    )(q, k, v, tri)


DIL_BLOCK = 128
DIL_SUB = 2


def _dil_kernel(q_ref, kp_ref, kc_ref, vp_ref, vc_ref, o_ref, lse_ref):
    i = pl.program_id(2)
    blk = DIL_BLOCK
    kall = jnp.concatenate([kp_ref[0], kc_ref[0]], axis=0)
    vall = jnp.concatenate([vp_ref[0], vc_ref[0]], axis=0)
    row = lax.broadcasted_iota(jnp.int32, (blk, 2 * blk), 0)
    col = lax.broadcasted_iota(jnp.int32, (blk, 2 * blk), 1)
    band = (col >= row) & (col <= row + blk)
    lane = lax.broadcasted_iota(jnp.int32, (blk, q_ref.shape[-1]), 1)
    for sb in range(DIL_SUB):
        q = q_ref[0, sb * blk:(sb + 1) * blk, :]
        k2 = kall[sb * blk:(sb + 2) * blk]
        v2 = vall[sb * blk:(sb + 2) * blk]
        valid = band & ((col >= blk) | (i > 0)) if sb == 0 else band
        out = jnp.zeros(q.shape, F32)
        lse_full = jnp.zeros(q.shape, F32)
        for h in range(DIL_HEADS_PER_GROUP):
            mine = (lane >= h * HEAD_DIM) & (lane < (h + 1) * HEAD_DIM)
            qh = jnp.where(mine, q, jnp.zeros_like(q))
            z = lax.dot_general(qh, k2, (((1,), (1,)), ((), ())), preferred_element_type=F32)
            z = jnp.where(valid, z, NEG_BIG)
            m = jnp.max(z, axis=-1, keepdims=True)
            p = jnp.exp(z - m)
            l = jnp.sum(p, axis=-1, keepdims=True)
            oh = jnp.dot(p.astype(BF16), v2, preferred_element_type=F32) / l
            out = jnp.where(mine, oh, out)
            lse_full = jnp.where(mine, m + jnp.log(l), lse_full)
        o_ref[0, sb * blk:(sb + 1) * blk, :] = out
        lse_ref[0, sb * blk:(sb + 1) * blk, :] = lse_full


def _dil_attention_group(q, k, v, g, r):
    b, s, w = q.shape
    sr = s // r
    blk = DIL_BLOCK
    gw = DIL_OUT_WIDTH
    per = w // gw
    qv, kv, vv = (a.reshape(b, sr, r * w) for a in (q, k, v))
    step = DIL_SUB * blk
    cur = pl.BlockSpec((1, step, gw), lambda bi, c, i: (bi, i, per * c + g))
    prev = pl.BlockSpec((1, blk, gw), lambda bi, c, i: (bi, jnp.maximum(DIL_SUB * i - 1, 0), per * c + g))
    ospec = pl.BlockSpec((1, step, gw), lambda bi, c, i: (bi, i, c))
    o, lse = pl.pallas_call(
        _dil_kernel,
        grid=(b, r, sr // step),
        in_specs=[cur, prev, cur, prev, cur],
        out_specs=[ospec, ospec],
        out_shape=[jax.ShapeDtypeStruct((b, sr, r * gw), F32)] * 2,
        compiler_params=_cparams(("parallel", "parallel", "arbitrary")),
        name=f"dil_attn_r{r}",
    )(qv, kv, kv, vv, vv)
    return o.reshape(b, s, gw), lse.reshape(b, s, gw)


def _merge_kernel(osb_ref, o0_ref, o1_ref, o2_ref, l0_ref, l1_ref, l2_ref, gate_ref, x_ref, mod_ref,
                  wsb_ref, wdil_ref, wout_ref, gffn_ref, wpq_ref, keys_ref,
                  x1_ref, h2_ref, sub_ref):
    d = x_ref.shape[-1]
    l0, l1, l2 = l0_ref[0], l1_ref[0], l2_ref[0]
    m = jnp.maximum(jnp.maximum(l0, l1), l2)
    e0, e1, e2 = jnp.exp(l0 - m), jnp.exp(l1 - m), jnp.exp(l2 - m)
    o_d = (e0 * o0_ref[0] + e1 * o1_ref[0] + e2 * o2_ref[0]) / (e0 + e1 + e2)
    br_sb = jnp.dot(osb_ref[0], wsb_ref[...], preferred_element_type=F32)
    br_d = jnp.dot(o_d.astype(BF16), wdil_ref[...], preferred_element_type=F32)
    merged = gate_ref[0, :, 0:d] * br_sb + gate_ref[0, :, d:2 * d] * br_d
    x1 = x_ref[0] + mod_ref[0, 2:3, :] * jnp.dot(merged.astype(BF16), wout_ref[...],
                                                 preferred_element_type=F32)
    x1_ref[0] = x1
    h2 = _rms_mod(x1, gffn_ref[...], mod_ref[0, 3:4, :], mod_ref[0, 4:5, :])
    h2_ref[0] = h2
    pq = jnp.dot(h2.astype(BF16), wpq_ref[...], preferred_element_type=F32).astype(BF16)
    for hp in range(2 * PEER_HEADS):
        sub_ref[hp] = lax.dot_general(keys_ref[hp], pq[:, hp * PEER_HALF:(hp + 1) * PEER_HALF],
                                      (((1,), (1,)), ((), ())), preferred_element_type=F32)


def _merge(o_sb, o_g, lse_g, gates, x, mod, w_sb_o, w_dil_o, w_out, g_ffn, w_pq, keys, tm=256):
    b, s, d = x.shape
    nt = s // tm
    tok = lambda w: pl.BlockSpec((1, tm, w), lambda bi, i: (bi, i, 0))
    nhp = 2 * PEER_HEADS
    return pl.pallas_call(
        _merge_kernel,
        grid=(b, nt),
        in_specs=[tok(SB_WIDTH)] + [tok(DIL_OUT_WIDTH)] * 6 + [tok(2 * d), tok(d),
                  pl.BlockSpec((1, N_MOD, d), lambda bi, i: (bi, 0, 0)),
                  _const_spec(w_sb_o.shape), _const_spec(w_dil_o.shape), _const_spec(w_out.shape),
                  _const_spec((1, d)), _const_spec(w_pq.shape), _const_spec(keys.shape)],
        out_specs=[tok(d), tok(d),
                   pl.BlockSpec((nhp, PEER_NKEYS, tm), lambda bi, i: (0, 0, bi * nt + i))],
        out_shape=[jax.ShapeDtypeStruct((b, s, d), F32), jax.ShapeDtypeStruct((b, s, d), F32),
                   jax.ShapeDtypeStruct((nhp, PEER_NKEYS, b * s), F32)],
        compiler_params=_cparams(("parallel", "parallel")),
        name="merge",
    )(o_sb, *o_g, *lse_g, gates, x, mod, w_sb_o, w_dil_o, w_out, g_ffn, w_pq, keys)


def _top_rows(s, k, payload=None):
    n = s.shape[0]
    rowid = lax.broadcasted_iota(jnp.int32, s.shape, 0).astype(F32)
    vals, sel = [], []
    for _ in range(k):
        m = jnp.max(s, axis=0, keepdims=True)
        ix = jnp.min(jnp.where(s == m, rowid, float(n)), axis=0, keepdims=True)
        hit = rowid == ix
        vals.append(m)
        sel.append(ix if payload is None else jnp.max(jnp.where(hit, payload, -1.0), axis=0, keepdims=True))
        s = jnp.where(hit, -jnp.inf, s)
    return jnp.concatenate(vals, axis=0), jnp.concatenate(sel, axis=0)


def _topk_kernel(sub_ref, ids_ref, gate_ref):
    k = PEER_TOPK
    tl = sub_ref.shape[-1]
    half = k // 2
    jrow = lax.broadcasted_iota(jnp.int32, (half, tl), 0)
    for h in range(PEER_HEADS):
        s0, i0 = _top_rows(sub_ref[2 * h], k)
        s1, i1 = _top_rows(sub_ref[2 * h + 1], k)
        cand = [s0[0:1, :] + s1]
        cand_e = [i0[0:1, :] * PEER_NKEYS + i1]
        for i in range(1, half):
            keep = jrow < (k // (i + 1))
            cand.append(jnp.where(keep, s0[i:i + 1, :] + s1[0:half, :], -jnp.inf))
            cand_e.append(i0[i:i + 1, :] * PEER_NKEYS + i1[0:half, :])
        cand.append(s0[half:k, :] + s1[0:1, :])
        cand_e.append(i0[half:k, :] * PEER_NKEYS + i1[0:1, :])
        best, expert = _top_rows(jnp.concatenate(cand, axis=0), k, jnp.concatenate(cand_e, axis=0))
        e = jnp.exp(best - best[0:1, :])
        ids_ref[h * k:(h + 1) * k, :] = expert.astype(jnp.int32)
        gate_ref[h * k:(h + 1) * k, :] = e / jnp.sum(e, axis=0, keepdims=True)


def _peer_topk(sub_t, tl=256):
    nhp, nk, t = sub_t.shape
    hk = PEER_HEADS * PEER_TOPK
    return pl.pallas_call(
        _topk_kernel,
        grid=(t // tl,),
        in_specs=[pl.BlockSpec((nhp, nk, tl), lambda i: (0, 0, i))],
        out_specs=[pl.BlockSpec((hk, tl), lambda i: (0, i))] * 2,
        out_shape=[jax.ShapeDtypeStruct((hk, t), jnp.int32), jax.ShapeDtypeStruct((hk, t), F32)],
        compiler_params=_cparams(("parallel",)),
        name="peer_topk",
    )(sub_t)


PEER_TT = 8
ROW_TILES = 8
DMA_THREADS = 2


def _peer_issue(ids_ref, uv_ref, buf, sem):
    for n in range(ids_ref.shape[-1]):
        src = uv_ref.at[ids_ref[0, 0, n]]
        dst = buf.at[:, n * ROW_TILES:(n + 1) * ROW_TILES, :]
        pltpu.make_async_copy(src, dst, sem).start(priority=n % DMA_THREADS)


def _peer_wait(buf, sem):
    pltpu.make_async_copy(buf, buf, sem).wait()


def _peer_compute(gate, x_ref, t0, gsum_ref, gexp_ref, buf, o_ref):
    hk = gate.shape[-1]
    rows = hk * ROW_TILES
    lane = lax.broadcasted_iota(jnp.int32, (ROW_TILES, rows), 1)
    sub = lax.broadcasted_iota(jnp.int32, (ROW_TILES, rows), 0)
    diag = (lane & (ROW_TILES - 1)) == sub
    zs = []
    for t in range(PEER_TT):
        xs = x_ref[t0 + t].astype(BF16)
        ucat = buf[0, t * rows:(t + 1) * rows, :].astype(BF16)
        y = lax.dot_general(xs, ucat, (((1,), (1,)), ((), ())), preferred_element_type=F32)
        zs.append(jnp.sum(jnp.where(diag, y, 0.0), axis=0, keepdims=True))
    z = jnp.concatenate(zs, axis=0)
    zh = z.astype(BF16)
    zl = (z - zh.astype(F32)).astype(BF16)
    act = (jnp.dot(zh, gsum_ref[...], preferred_element_type=F32)
           + jnp.dot(zl, gsum_ref[...], preferred_element_type=F32))
    c = (gate * jax.nn.gelu(act)).astype(BF16)
    crep = jnp.dot(c, gexp_ref[...], preferred_element_type=F32)
    for t in range(PEER_TT):
        cm = jnp.where(diag, jnp.broadcast_to(crep[t:t + 1, :], diag.shape), 0.0).astype(BF16)
        vcat = buf[1, t * rows:(t + 1) * rows, :].astype(BF16)
        o_ref[t0 + t] = jnp.dot(cm, vcat, preferred_element_type=F32)


def _peer_kernel(ids_a_ref, ids_b_ref, ids_next_ref, gate_ref, x_ref, gsum_ref, gexp_ref, uv_ref, o_ref,
                 buf_a, buf_b, sem_a, sem_b):
    i = pl.program_id(0)
    last = pl.num_programs(0) - 1

    @pl.when(i == 0)
    def _():
        _peer_issue(ids_a_ref, uv_ref, buf_a, sem_a)

    _peer_issue(ids_b_ref, uv_ref, buf_b, sem_b)
    _peer_wait(buf_a, sem_a)
    _peer_compute(gate_ref[0:PEER_TT, :], x_ref, 0, gsum_ref, gexp_ref, buf_a, o_ref)
    _peer_issue(ids_next_ref, uv_ref, buf_a, sem_a)
    _peer_wait(buf_b, sem_b)
    _peer_compute(gate_ref[PEER_TT:2 * PEER_TT, :], x_ref, PEER_TT, gsum_ref, gexp_ref, buf_b, o_ref)

    @pl.when(i == last)
    def _():
        _peer_wait(buf_a, sem_a)


def _peer_ffn(ids, gate, h2_tiles, uv):
    t, hk = ids.shape
    rows = hk * ROW_TILES
    r = lax.broadcasted_iota(jnp.int32, (rows, hk), 0)
    c = lax.broadcasted_iota(jnp.int32, (rows, hk), 1)
    gsum = (r // ROW_TILES == c).astype(BF16)
    gexp = gsum.T
    tt = PEER_TT
    nblk = t // tt
    ids3 = ids.reshape(nblk, 1, tt * hk)
    smem = lambda f: pl.BlockSpec((1, 1, tt * hk), f, memory_space=pltpu.SMEM)
    buf = pltpu.VMEM((2, tt * rows, LANES), F32)
    return pl.pallas_call(
        _peer_kernel,
        grid=(nblk // 2,),
        in_specs=[smem(lambda i: (2 * i, 0, 0)), smem(lambda i: (2 * i + 1, 0, 0)),
                  smem(lambda i: (jnp.minimum(2 * i + 2, nblk - 2), 0, 0)),
                  pl.BlockSpec((2 * tt, hk), lambda i: (i, 0)),
                  pl.BlockSpec((2 * tt, ROW_TILES, LANES), lambda i: (i, 0, 0)),
                  _const_spec((rows, hk)), _const_spec((hk, rows)),
                  pl.BlockSpec(memory_space=pl.ANY)],
        out_specs=pl.BlockSpec((2 * tt, ROW_TILES, LANES), lambda i: (i, 0, 0)),
        out_shape=jax.ShapeDtypeStruct((t, ROW_TILES, LANES), F32),
        scratch_shapes=[buf, buf, pltpu.SemaphoreType.DMA(()), pltpu.SemaphoreType.DMA(())],
        compiler_params=_cparams(("arbitrary",)),
        name="peer_ffn",
    )(ids3, ids3, ids3, gate, h2_tiles, gsum, gexp, uv)


def _final_kernel(x1_ref, f_ref, mod_ref, g_ref, o_ref):
    x2 = x1_ref[0] + mod_ref[0, 5:6, :] * f_ref[0]
    ms = jnp.mean(x2 * x2, axis=-1, keepdims=True)
    o_ref[0] = x2 * lax.rsqrt(ms + EPS) * g_ref[...]


def _final(x1, ffn, mod, g_final, tm=512):
    b, s, d = x1.shape
    tok = pl.BlockSpec((1, tm, d), lambda bi, i: (bi, i, 0))
    return pl.pallas_call(
        _final_kernel,
        grid=(b, s // tm),
        in_specs=[tok, tok, pl.BlockSpec((1, N_MOD, d), lambda bi, i: (bi, 0, 0)), _const_spec((1, d))],
        out_specs=tok,
        out_shape=jax.ShapeDtypeStruct((b, s, d), F32),
        compiler_params=_cparams(("parallel", "parallel")),
        name="final",
    )(x1, ffn, mod, g_final)


def _rope_tables(s):
    half = HEAD_DIM // 2
    inv = ROPE_THETA ** (-jnp.arange(half, dtype=F32) / half)
    ang = jnp.arange(s, dtype=F32)[:, None] * inv[None, :]
    cos, sin = jnp.cos(ang), jnp.sin(ang)
    reps = LANES // HEAD_DIM
    return (jnp.tile(jnp.concatenate([cos, cos], axis=1), (1, reps)),
            jnp.tile(jnp.concatenate([-sin, sin], axis=1), (1, reps)))


def kernel(x, c, w_ada, b_ada, g_mix, w_in, w_sb_o, w_dil_o, w_out, g_ffn, w_pq, peer_keys, peer_u, peer_v,
           g_final):
    b, s, d = x.shape
    assert w_ada.shape[0] == 1, "the final residual and norm are fused: single layer only"
    layer = 0
    cos, sin = _rope_tables(s)
    mod = _adaln(c, w_ada[layer], b_ada[layer]).reshape(b, N_MOD, d)
    q_sb, k_sb, v_sb, q_d, k_d, v_d, gates = _inproj(
        x, mod, g_mix[layer].reshape(1, d), w_in[layer].astype(BF16), cos, sin)
    o_sb = _sb_attention(q_sb, k_sb, v_sb)
    o_g, lse_g = zip(*[_dil_attention_group(q_d, k_d, v_d, g, r) for g, (_, r) in enumerate(DIL_GROUPS)])
    keys = peer_keys[layer].reshape(2 * PEER_HEADS, PEER_NKEYS, PEER_HALF).astype(BF16)
    x1, h2, sub_t = _merge(o_sb, o_g, lse_g, gates, x, mod,
                           w_sb_o[layer].astype(BF16), w_dil_o[layer].astype(BF16),
                           w_out[layer].astype(BF16), g_ffn[layer].reshape(1, d),
                           w_pq[layer].astype(BF16), keys)
    ids_t, gate_t = _peer_topk(sub_t)
    n_exp = peer_u.shape[1]
    uv = jnp.stack([peer_u[layer].reshape(n_exp, ROW_TILES, LANES),
                    peer_v[layer].reshape(n_exp, ROW_TILES, LANES)], axis=1)
    ffn = _peer_ffn(ids_t.T, gate_t.T, h2.reshape(b * s, ROW_TILES, LANES), uv)
    return _final(x1, ffn.reshape(b, s, d), mod, g_final.reshape(1, d))
```

```python
import functools

import jax
import jax.numpy as jnp
from jax import lax
from jax.experimental import pallas as pl
from jax.experimental.pallas import tpu as pltpu

F32 = jnp.float32
BF16 = jnp.bfloat16

HEAD_DIM = 64
SB_HEADS = 8
DIL_GROUPS = ((128, 1), (512, 4), (2048, 16))
DIL_HEADS_PER_GROUP = 4
SB_WIDTH = SB_HEADS * HEAD_DIM
DIL_WIDTH = len(DIL_GROUPS) * DIL_HEADS_PER_GROUP * HEAD_DIM
DIL_OUT_WIDTH = DIL_HEADS_PER_GROUP * HEAD_DIM
ROPE_THETA = 10000.0
PEER_HEADS = 8
PEER_NKEYS = 128
PEER_HALF = 128
PEER_TOPK = 16
N_MOD = 6
EPS = 1e-6

LANES = 128
SUBLANES = 8
VMEM_LIMIT = 56 * 1024 * 1024
NEG_BIG = -1e30


def _cparams(sem):
    return pltpu.CompilerParams(dimension_semantics=sem, vmem_limit_bytes=VMEM_LIMIT)


def _const_spec(shape):
    nd = len(shape)
    return pl.BlockSpec(shape, lambda *_: (0,) * nd, pipeline_mode=pl.Buffered(1))


def _adaln_kernel(c_ref, w_ref, b_ref, o_ref):
    c = c_ref[...]
    s = c * jax.nn.sigmoid(c)
    o_ref[...] = jnp.dot(s, w_ref[...], preferred_element_type=F32) + b_ref[...]


def _adaln(c, w, b):
    bsz, d = c.shape
    n = w.shape[1]
    bn = 1536
    return pl.pallas_call(
        _adaln_kernel,
        grid=(n // bn,),
        in_specs=[pl.BlockSpec((bsz, d), lambda j: (0, 0)),
                  pl.BlockSpec((d, bn), lambda j: (0, j)),
                  pl.BlockSpec((1, bn), lambda j: (0, j))],
        out_specs=pl.BlockSpec((bsz, bn), lambda j: (0, j)),
        out_shape=jax.ShapeDtypeStruct((bsz, n), F32),
        compiler_params=_cparams(("arbitrary",)),
        name="adaln",
    )(c, w, b.reshape(1, n))


def _rms_mod(x, g, shift, scale):
    ms = jnp.mean(x * x, axis=-1, keepdims=True)
    y = x * lax.rsqrt(ms + EPS) * g
    return y * (1.0 + scale) + shift


def _rope(x, cos, sin):
    lane = lax.broadcasted_iota(jnp.int32, x.shape, 1)
    first_half = (lane & (HEAD_DIM - 1)) < (HEAD_DIM // 2)
    swapped = jnp.where(first_half,
                        pltpu.roll(x, LANES - HEAD_DIM // 2, axis=1),
                        pltpu.roll(x, HEAD_DIM // 2, axis=1))
    return x * cos + swapped * sin


def _inproj_kernel(x_ref, mod_ref, g_ref, w_ref, cos_ref, sin_ref,
                   qs_ref, ks_ref, vs_ref, qd_ref, kd_ref, vd_ref, gate_ref):
    x = x_ref[0]
    h = _rms_mod(x, g_ref[...], mod_ref[0, 0:1, :], mod_ref[0, 1:2, :]).astype(BF16)
    scale = HEAD_DIM ** -0.5

    def proj(lo, n):
        return jnp.dot(h, w_ref[:, lo:lo + n], preferred_element_type=F32)

    o = 0
    qs_ref[0] = (proj(o, SB_WIDTH) * scale).astype(BF16); o += SB_WIDTH
    ks_ref[0] = proj(o, SB_WIDTH).astype(BF16); o += SB_WIDTH
    vs_ref[0] = proj(o, SB_WIDTH).astype(BF16); o += SB_WIDTH
    cos = cos_ref[...]
    sin = sin_ref[...]
    for ref, sc in ((qd_ref, scale), (kd_ref, 1.0)):
        for j in range(DIL_WIDTH // LANES):
            p = proj(o + j * LANES, LANES)
            ref[0, :, j * LANES:(j + 1) * LANES] = (_rope(p, cos, sin) * sc).astype(BF16)
        o += DIL_WIDTH
    vd_ref[0] = proj(o, DIL_WIDTH).astype(BF16); o += DIL_WIDTH
    d = x.shape[-1]
    gate_ref[0] = jax.nn.sigmoid(proj(o, 2 * d))


def _inproj(x, mod, g_mix, w_in_bf, cos, sin, tm=256):
    b, s, d = x.shape
    ncols = w_in_bf.shape[1]
    tok = lambda w: pl.BlockSpec((1, tm, w), lambda bi, i: (bi, i, 0))
    outs = [(SB_WIDTH, BF16)] * 3 + [(DIL_WIDTH, BF16)] * 3 + [(2 * d, F32)]
    return pl.pallas_call(
        _inproj_kernel,
        grid=(b, s // tm),
        in_specs=[tok(d),
                  pl.BlockSpec((1, N_MOD, d), lambda bi, i: (bi, 0, 0)),
                  _const_spec((1, d)),
                  _const_spec((d, ncols)),
                  pl.BlockSpec((tm, LANES), lambda bi, i: (i, 0)),
                  pl.BlockSpec((tm, LANES), lambda bi, i: (i, 0))],
        out_specs=[tok(w) for w, _ in outs],
        out_shape=[jax.ShapeDtypeStruct((b, s, w), dt) for w, dt in outs],
        compiler_params=_cparams(("parallel", "parallel")),
        name="inproj",
    )(x, mod, g_mix, w_in_bf, cos, sin)


SB_BLOCK = 256
F32_EXP_UNDERFLOW = -104.0


def _neg_softplus(z):
    return -(jnp.maximum(z, 0.0) + jnp.log(1.0 + jnp.exp(-jnp.abs(z))))


def _sb_kernel(q_ref, k_ref, v_ref, tri_ref, o_ref):
    i = pl.program_id(2)
    blk = SB_BLOCK
    q = q_ref[0]
    lane = lax.broadcasted_iota(jnp.int32, q.shape, 1)
    q2 = jnp.concatenate([jnp.where(lane < HEAD_DIM, q, jnp.zeros_like(q)),
                          jnp.where(lane >= HEAD_DIM, q, jnp.zeros_like(q))], axis=0)
    tri = tri_ref[...]

    def block(kb, carry, causal):
        run, acc = carry
        start = pl.multiple_of(kb * blk, blk)
        k = k_ref[0, pl.ds(start, blk), :]
        v = v_ref[0, pl.ds(start, blk), :]
        z = lax.dot_general(q2, k, (((1,), (1,)), ((), ())), preferred_element_type=F32)
        lsm = _neg_softplus(z)
        if causal:
            row = lax.broadcasted_iota(jnp.int32, z.shape, 0) & (blk - 1)
            col = lax.broadcasted_iota(jnp.int32, z.shape, 1)
            valid = col < row
            lsm = jnp.where(valid, lsm, 0.0)
        hi = lsm.astype(BF16)
        lo = (lsm - hi.astype(F32)).astype(BF16)
        cs = (jnp.dot(hi, tri, preferred_element_type=F32)
              + jnp.dot(lo, tri, preferred_element_type=F32))
        a = jnp.exp(z + cs + run)
        if causal:
            a = jnp.where(valid, a, 0.0)
        acc = acc + jnp.dot(a.astype(BF16), v, preferred_element_type=F32)
        return run + cs[:, 0:1], acc

    def alive(carry):
        top = jnp.max(carry[0], axis=0, keepdims=True)
        return (top[0, 0] >= F32_EXP_UNDERFLOW).astype(jnp.int32)

    first = block(i, (jnp.zeros((2 * blk, 1), F32), jnp.zeros((2 * blk, LANES), F32)), causal=True)

    def cond(state):
        return (state[0] >= 0) & (state[1] > 0)

    def body(state):
        kb, _, carry = state
        carry = block(kb, carry, causal=False)
        return kb - 1, alive(carry), carry

    acc = lax.while_loop(cond, body, (i - 1, alive(first), first))[2][1]
    o_ref[0] = jnp.where(lane < HEAD_DIM, acc[0:blk], acc[blk:2 * blk]).astype(o_ref.dtype)


def _sb_attention(q, k, v):
    b, s, w = q.shape
    blk = SB_BLOCK
    r = lax.broadcasted_iota(jnp.int32, (blk, blk), 0)
    c = lax.broadcasted_iota(jnp.int32, (blk, blk), 1)
    tri = (r >= c).astype(BF16)
    return pl.pallas_call(
        _sb_kernel,
        grid=(b, w // LANES, s // blk),
        in_specs=[pl.BlockSpec((1, blk, LANES), lambda bi, p, i: (bi, i, p)),
                  pl.BlockSpec((1, s, LANES), lambda bi, p, i: (bi, 0, p)),
                  pl.BlockSpec((1, s, LANES), lambda bi, p, i: (bi, 0, p)),
                  _const_spec((blk, blk))],
        out_specs=pl.BlockSpec((1, blk, LANES), lambda bi, p, i: (bi, i, p)),
        out_shape=jax.ShapeDtypeStruct((b, s, w), BF16),
        compiler_params=_cparams(("parallel", "parallel", "arbitrary")),
        name="sb_attn",
    )(q, k, v, tri)


DIL_BLOCK = 128
DIL_SUB = 2


def _dil_kernel(q_ref, kp_ref, kc_ref, vp_ref, vc_ref, o_ref, lse_ref):
    i = pl.program_id(2)
    blk = DIL_BLOCK
    kall = jnp.concatenate([kp_ref[0], kc_ref[0]], axis=0)
    vall = jnp.concatenate([vp_ref[0], vc_ref[0]], axis=0)
    nh = DIL_HEADS_PER_GROUP
    row = lax.broadcasted_iota(jnp.int32, (nh * blk, 2 * blk), 0) & (blk - 1)
    col = lax.broadcasted_iota(jnp.int32, (nh * blk, 2 * blk), 1)
    band = (col >= row) & (col <= row + blk)
    lane = lax.broadcasted_iota(jnp.int32, (blk, q_ref.shape[-1]), 1)
    mine = [(lane >= h * HEAD_DIM) & (lane < (h + 1) * HEAD_DIM) for h in range(nh)]
    for sb in range(DIL_SUB):
        q = q_ref[0, sb * blk:(sb + 1) * blk, :]
        k2 = kall[sb * blk:(sb + 2) * blk]
        v2 = vall[sb * blk:(sb + 2) * blk]
        valid = band & ((col >= blk) | (i > 0)) if sb == 0 else band
        q4 = jnp.concatenate([jnp.where(mine[h], q, jnp.zeros_like(q)) for h in range(nh)], axis=0)
        z = lax.dot_general(q4, k2, (((1,), (1,)), ((), ())), preferred_element_type=F32)
        z = jnp.where(valid, z, NEG_BIG)
        m = jnp.max(z, axis=-1, keepdims=True)
        p = jnp.exp(z - m)
        l = jnp.sum(p, axis=-1, keepdims=True)
        o4 = jnp.dot(p.astype(BF16), v2, preferred_element_type=F32) / l
        lse4 = m + jnp.log(l)
        out = jnp.zeros(q.shape, F32)
        lse_full = jnp.zeros(q.shape, F32)
        for h in range(nh):
            out = jnp.where(mine[h], o4[h * blk:(h + 1) * blk], out)
            lse_full = jnp.where(mine[h], lse4[h * blk:(h + 1) * blk], lse_full)
        o_ref[0, sb * blk:(sb + 1) * blk, :] = out
        lse_ref[0, sb * blk:(sb + 1) * blk, :] = lse_full


def _dil_attention_group(q, k, v, g, r):
    b, s, w = q.shape
    sr = s // r
    blk = DIL_BLOCK
    gw = DIL_OUT_WIDTH
    per = w // gw
    qv, kv, vv = (a.reshape(b, sr, r * w) for a in (q, k, v))
    step = DIL_SUB * blk
    cur = pl.BlockSpec((1, step, gw), lambda bi, c, i: (bi, i, per * c + g))
    prev = pl.BlockSpec((1, blk, gw), lambda bi, c, i: (bi, jnp.maximum(DIL_SUB * i - 1, 0), per * c + g))
    ospec = pl.BlockSpec((1, step, gw), lambda bi, c, i: (bi, i, c))
    o, lse = pl.pallas_call(
        _dil_kernel,
        grid=(b, r, sr // step),
        in_specs=[cur, prev, cur, prev, cur],
        out_specs=[ospec, ospec],
        out_shape=[jax.ShapeDtypeStruct((b, sr, r * gw), F32)] * 2,
        compiler_params=_cparams(("parallel", "parallel", "arbitrary")),
        name=f"dil_attn_r{r}",
    )(qv, kv, kv, vv, vv)
    return o.reshape(b, s, gw), lse.reshape(b, s, gw)


def _merge_kernel(osb_ref, o0_ref, o1_ref, o2_ref, l0_ref, l1_ref, l2_ref, gate_ref, x_ref, mod_ref,
                  wsb_ref, wdil_ref, wout_ref, gffn_ref, wpq_ref, keys_ref,
                  x1_ref, h2_ref, sub_ref):
    d = x_ref.shape[-1]
    l0, l1, l2 = l0_ref[0], l1_ref[0], l2_ref[0]
    m = jnp.maximum(jnp.maximum(l0, l1), l2)
    e0, e1, e2 = jnp.exp(l0 - m), jnp.exp(l1 - m), jnp.exp(l2 - m)
    o_d = (e0 * o0_ref[0] + e1 * o1_ref[0] + e2 * o2_ref[0]) / (e0 + e1 + e2)
    br_sb = jnp.dot(osb_ref[0], wsb_ref[...], preferred_element_type=F32)
    br_d = jnp.dot(o_d.astype(BF16), wdil_ref[...], preferred_element_type=F32)
    merged = gate_ref[0, :, 0:d] * br_sb + gate_ref[0, :, d:2 * d] * br_d
    x1 = x_ref[0] + mod_ref[0, 2:3, :] * jnp.dot(merged.astype(BF16), wout_ref[...],
                                                 preferred_element_type=F32)
    x1_ref[0] = x1
    h2 = _rms_mod(x1, gffn_ref[...], mod_ref[0, 3:4, :], mod_ref[0, 4:5, :])
    h2_ref[0] = h2
    pq = jnp.dot(h2.astype(BF16), wpq_ref[...], preferred_element_type=F32).astype(BF16)
    for hp in range(2 * PEER_HEADS):
        sub_ref[hp] = lax.dot_general(keys_ref[hp], pq[:, hp * PEER_HALF:(hp + 1) * PEER_HALF],
                                      (((1,), (1,)), ((), ())), preferred_element_type=F32)


def _merge(o_sb, o_g, lse_g, gates, x, mod, w_sb_o, w_dil_o, w_out, g_ffn, w_pq, keys, tm=256):
    b, s, d = x.shape
    nt = s // tm
    tok = lambda w: pl.BlockSpec((1, tm, w), lambda bi, i: (bi, i, 0))
    nhp = 2 * PEER_HEADS
    return pl.pallas_call(
        _merge_kernel,
        grid=(b, nt),
        in_specs=[tok(SB_WIDTH)] + [tok(DIL_OUT_WIDTH)] * 6 + [tok(2 * d), tok(d),
                  pl.BlockSpec((1, N_MOD, d), lambda bi, i: (bi, 0, 0)),
                  _const_spec(w_sb_o.shape), _const_spec(w_dil_o.shape), _const_spec(w_out.shape),
                  _const_spec((1, d)), _const_spec(w_pq.shape), _const_spec(keys.shape)],
        out_specs=[tok(d), tok(d),
                   pl.BlockSpec((nhp, PEER_NKEYS, tm), lambda bi, i: (0, 0, bi * nt + i))],
        out_shape=[jax.ShapeDtypeStruct((b, s, d), F32), jax.ShapeDtypeStruct((b, s, d), F32),
                   jax.ShapeDtypeStruct((nhp, PEER_NKEYS, b * s), F32)],
        compiler_params=_cparams(("parallel", "parallel")),
        name="merge",
    )(o_sb, *o_g, *lse_g, gates, x, mod, w_sb_o, w_dil_o, w_out, g_ffn, w_pq, keys)


def _top_rows(s, k, payload=None):
    n = s.shape[0]
    rowid = lax.broadcasted_iota(jnp.int32, s.shape, 0).astype(F32)
    vals, sel = [], []
    for _ in range(k):
        m = jnp.max(s, axis=0, keepdims=True)
        ix = jnp.min(jnp.where(s == m, rowid, float(n)), axis=0, keepdims=True)
        hit = rowid == ix
        vals.append(m)
        sel.append(ix if payload is None else jnp.max(jnp.where(hit, payload, -1.0), axis=0, keepdims=True))
        s = jnp.where(hit, -jnp.inf, s)
    return jnp.concatenate(vals, axis=0), jnp.concatenate(sel, axis=0)


def _topk_kernel(sub_ref, ids_ref, gate_ref):
    k = PEER_TOPK
    tl = sub_ref.shape[-1]
    half = k // 2
    jrow = lax.broadcasted_iota(jnp.int32, (half, tl), 0)
    for h in range(PEER_HEADS):
        s0, i0 = _top_rows(sub_ref[2 * h], k)
        s1, i1 = _top_rows(sub_ref[2 * h + 1], k)
        cand = [s0[0:1, :] + s1]
        cand_e = [i0[0:1, :] * PEER_NKEYS + i1]
        for i in range(1, half):
            keep = jrow < (k // (i + 1))
            cand.append(jnp.where(keep, s0[i:i + 1, :] + s1[0:half, :], -jnp.inf))
            cand_e.append(i0[i:i + 1, :] * PEER_NKEYS + i1[0:half, :])
        cand.append(s0[half:k, :] + s1[0:1, :])
        cand_e.append(i0[half:k, :] * PEER_NKEYS + i1[0:1, :])
        best, expert = _top_rows(jnp.concatenate(cand, axis=0), k, jnp.concatenate(cand_e, axis=0))
        e = jnp.exp(best - best[0:1, :])
        ids_ref[h * k:(h + 1) * k, :] = expert.astype(jnp.int32)
        gate_ref[h * k:(h + 1) * k, :] = e / jnp.sum(e, axis=0, keepdims=True)


def _peer_topk(sub_t, tl=256):
    nhp, nk, t = sub_t.shape
    hk = PEER_HEADS * PEER_TOPK
    return pl.pallas_call(
        _topk_kernel,
        grid=(t // tl,),
        in_specs=[pl.BlockSpec((nhp, nk, tl), lambda i: (0, 0, i))],
        out_specs=[pl.BlockSpec((hk, tl), lambda i: (0, i))] * 2,
        out_shape=[jax.ShapeDtypeStruct((hk, t), jnp.int32), jax.ShapeDtypeStruct((hk, t), F32)],
        compiler_params=_cparams(("parallel",)),
        name="peer_topk",
    )(sub_t)


PEER_TT = 8
PEER_BUFS = 4
PEER_AHEAD = 2
ROW_TILES = 8
PAIR_ROWS = 2 * ROW_TILES
DMA_THREADS = 2
ISSUE_UNROLL = 16


def _peer_issue(ids_ref, uv_ref, buf, sem):
    for n in range(ids_ref.shape[-1]):
        dst = buf.at[n * PAIR_ROWS:(n + 1) * PAIR_ROWS, :]
        pltpu.make_async_copy(uv_ref.at[ids_ref[0, 0, n]], dst, sem).start(priority=n % DMA_THREADS)


def _peer_issue_rolled(ids_ref, uv_ref, buf, sem):
    def chunk(j, _):
        for kk in range(ISSUE_UNROLL):
            n = j * ISSUE_UNROLL + kk
            dst = buf.at[pl.ds(pl.multiple_of(n * PAIR_ROWS, PAIR_ROWS), PAIR_ROWS), :]
            pltpu.make_async_copy(uv_ref.at[ids_ref[0, 0, n]], dst, sem).start(priority=kk % DMA_THREADS)
        return 0

    lax.fori_loop(0, ids_ref.shape[-1] // ISSUE_UNROLL, chunk, 0)


def _peer_wait(buf, sem):
    pltpu.make_async_copy(buf, buf, sem).wait()


def _peer_compute(gate, x_ref, t0, gsum_ref, gexp_ref, buf, o_ref):
    hk = gate.shape[-1]
    rows = hk * PAIR_ROWS
    lane = lax.broadcasted_iota(jnp.int32, (ROW_TILES, rows), 1)
    sub = lax.broadcasted_iota(jnp.int32, (ROW_TILES, rows), 0)
    tile_row = lane & (PAIR_ROWS - 1)
    diag_u = tile_row == sub
    diag_v = tile_row == sub + ROW_TILES
    zs = []
    for t in range(PEER_TT):
        xs = x_ref[t0 + t].astype(BF16)
        uvt = buf[t * rows:(t + 1) * rows, :]
        y = lax.dot_general(xs, uvt, (((1,), (1,)), ((), ())), preferred_element_type=F32)
        zs.append(jnp.sum(jnp.where(diag_u, y, 0.0), axis=0, keepdims=True))
    z = jnp.concatenate(zs, axis=0)
    zh = z.astype(BF16)
    zl = (z - zh.astype(F32)).astype(BF16)
    act = (jnp.dot(zh, gsum_ref[...], preferred_element_type=F32)
           + jnp.dot(zl, gsum_ref[...], preferred_element_type=F32))
    c = (gate * jax.nn.gelu(act)).astype(BF16)
    crep = jnp.dot(c, gexp_ref[...], preferred_element_type=F32)
    for t in range(PEER_TT):
        cm = jnp.where(diag_v, jnp.broadcast_to(crep[t:t + 1, :], diag_v.shape), 0.0).astype(BF16)
        o_ref[t0 + t] = jnp.dot(cm, buf[t * rows:(t + 1) * rows, :], preferred_element_type=F32)


def _peer_kernel(*refs):
    nb = PEER_BUFS
    ids_now = refs[:nb]
    ids_nxt = refs[nb:nb + PEER_AHEAD]
    gate_ref, x_ref, gsum_ref, gexp_ref, uv_ref, o_ref = refs[nb + PEER_AHEAD:nb + PEER_AHEAD + 6]
    bufs = refs[nb + PEER_AHEAD + 6:2 * nb + PEER_AHEAD + 6]
    sems = refs[2 * nb + PEER_AHEAD + 6:]
    i = pl.program_id(0)

    @pl.when(i == 0)
    def _():
        for p in range(PEER_AHEAD):
            _peer_issue_rolled(ids_now[p], uv_ref, bufs[p], sems[p])

    for p in range(nb):
        _peer_wait(bufs[p], sems[p])
        q = p + PEER_AHEAD
        ids_q = ids_now[q] if q < nb else ids_nxt[q - nb]
        _peer_issue(ids_q, uv_ref, bufs[q % nb], sems[q % nb])
        _peer_compute(gate_ref[p * PEER_TT:(p + 1) * PEER_TT, :], x_ref, p * PEER_TT, gsum_ref, gexp_ref,
                      bufs[p], o_ref)

    @pl.when(i == pl.num_programs(0) - 1)
    def _():
        for p in range(PEER_AHEAD):
            _peer_wait(bufs[p], sems[p])


def _peer_ffn(ids, gate, h2_tiles, uv):
    t, hk = ids.shape
    rows = hk * PAIR_ROWS
    r = lax.broadcasted_iota(jnp.int32, (rows, hk), 0)
    c = lax.broadcasted_iota(jnp.int32, (rows, hk), 1)
    gsum = (r // PAIR_ROWS == c).astype(BF16)
    gexp = gsum.T
    tt, nb = PEER_TT, PEER_BUFS
    nblk = t // tt
    nstep = nblk // nb
    ids3 = ids.reshape(nblk, 1, tt * hk)
    smem = lambda f: pl.BlockSpec((1, 1, tt * hk), f, memory_space=pltpu.SMEM)
    now = [smem(functools.partial(lambda i, p: (nb * i + p, 0, 0), p=p)) for p in range(nb)]
    nxt = [smem(functools.partial(lambda i, p: (nb * jnp.minimum(i + 1, nstep - 1) + p, 0, 0), p=p))
           for p in range(PEER_AHEAD)]
    return pl.pallas_call(
        _peer_kernel,
        grid=(nstep,),
        in_specs=now + nxt + [
            pl.BlockSpec((nb * tt, hk), lambda i: (i, 0)),
            pl.BlockSpec((nb * tt, ROW_TILES, LANES), lambda i: (i, 0, 0)),
            _const_spec((rows, hk)), _const_spec((hk, rows)),
            pl.BlockSpec(memory_space=pl.ANY)],
        out_specs=pl.BlockSpec((nb * tt, ROW_TILES, LANES), lambda i: (i, 0, 0)),
        out_shape=jax.ShapeDtypeStruct((t, ROW_TILES, LANES), F32),
        scratch_shapes=[pltpu.VMEM((tt * rows, LANES), BF16)] * nb + [pltpu.SemaphoreType.DMA(())] * nb,
        compiler_params=_cparams(("arbitrary",)),
        name="peer_ffn",
    )(*([ids3] * (nb + PEER_AHEAD)), gate, h2_tiles, gsum, gexp, uv)


def _final_kernel(x1_ref, f_ref, mod_ref, g_ref, o_ref):
    x2 = x1_ref[0] + mod_ref[0, 5:6, :] * f_ref[0]
    ms = jnp.mean(x2 * x2, axis=-1, keepdims=True)
    o_ref[0] = x2 * lax.rsqrt(ms + EPS) * g_ref[...]


def _final(x1, ffn, mod, g_final, tm=512):
    b, s, d = x1.shape
    tok = pl.BlockSpec((1, tm, d), lambda bi, i: (bi, i, 0))
    return pl.pallas_call(
        _final_kernel,
        grid=(b, s // tm),
        in_specs=[tok, tok, pl.BlockSpec((1, N_MOD, d), lambda bi, i: (bi, 0, 0)), _const_spec((1, d))],
        out_specs=tok,
        out_shape=jax.ShapeDtypeStruct((b, s, d), F32),
        compiler_params=_cparams(("parallel", "parallel")),
        name="final",
    )(x1, ffn, mod, g_final)


def _rope_tables(s):
    half = HEAD_DIM // 2
    inv = ROPE_THETA ** (-jnp.arange(half, dtype=F32) / half)
    ang = jnp.arange(s, dtype=F32)[:, None] * inv[None, :]
    cos, sin = jnp.cos(ang), jnp.sin(ang)
    reps = LANES // HEAD_DIM
    return (jnp.tile(jnp.concatenate([cos, cos], axis=1), (1, reps)),
            jnp.tile(jnp.concatenate([-sin, sin], axis=1), (1, reps)))


def kernel(x, c, w_ada, b_ada, g_mix, w_in, w_sb_o, w_dil_o, w_out, g_ffn, w_pq, peer_keys, peer_u, peer_v,
           g_final):
    b, s, d = x.shape
    assert w_ada.shape[0] == 1, "the final residual and norm are fused: single layer only"
    layer = 0
    cos, sin = _rope_tables(s)
    mod = _adaln(c, w_ada[layer], b_ada[layer]).reshape(b, N_MOD, d)
    q_sb, k_sb, v_sb, q_d, k_d, v_d, gates = _inproj(
        x, mod, g_mix[layer].reshape(1, d), w_in[layer].astype(BF16), cos, sin)
    o_sb = _sb_attention(q_sb, k_sb, v_sb)
    o_g, lse_g = zip(*[_dil_attention_group(q_d, k_d, v_d, g, r) for g, (_, r) in enumerate(DIL_GROUPS)])
    keys = peer_keys[layer].reshape(2 * PEER_HEADS, PEER_NKEYS, PEER_HALF).astype(BF16)
    x1, h2, sub_t = _merge(o_sb, o_g, lse_g, gates, x, mod,
                           w_sb_o[layer].astype(BF16), w_dil_o[layer].astype(BF16),
                           w_out[layer].astype(BF16), g_ffn[layer].reshape(1, d),
                           w_pq[layer].astype(BF16), keys)
    ids_t, gate_t = _peer_topk(sub_t)
    n_exp = peer_u.shape[1]
    uv = jnp.concatenate([peer_u[layer].reshape(n_exp, ROW_TILES, LANES),
                          peer_v[layer].reshape(n_exp, ROW_TILES, LANES)], axis=1).astype(BF16)
    ffn = _peer_ffn(ids_t.T, gate_t.T, h2.reshape(b * s, ROW_TILES, LANES), uv)
    return _final(x1, ffn.reshape(b, s, d), mod, g_final.reshape(1, d))
```

```python
import functools

import jax
import jax.numpy as jnp
from jax import lax
from jax.experimental import pallas as pl
from jax.experimental.pallas import tpu as pltpu

F32 = jnp.float32
BF16 = jnp.bfloat16

HEAD_DIM = 64
SB_HEADS = 8
DIL_GROUPS = ((128, 1), (512, 4), (2048, 16))
DIL_HEADS_PER_GROUP = 4
SB_WIDTH = SB_HEADS * HEAD_DIM
DIL_WIDTH = len(DIL_GROUPS) * DIL_HEADS_PER_GROUP * HEAD_DIM
DIL_OUT_WIDTH = DIL_HEADS_PER_GROUP * HEAD_DIM
ROPE_THETA = 10000.0
PEER_HEADS = 8
PEER_NKEYS = 128
PEER_HALF = 128
PEER_TOPK = 16
N_MOD = 6
EPS = 1e-6

LANES = 128
SUBLANES = 8
VMEM_LIMIT = 56 * 1024 * 1024
NEG_BIG = -1e30


def _cparams(sem):
    return pltpu.CompilerParams(dimension_semantics=sem, vmem_limit_bytes=VMEM_LIMIT)


def _const_spec(shape):
    nd = len(shape)
    return pl.BlockSpec(shape, lambda *_: (0,) * nd, pipeline_mode=pl.Buffered(1))


def _adaln_kernel(c_ref, w_ref, b_ref, o_ref):
    c = c_ref[...]
    s = c * jax.nn.sigmoid(c)
    o_ref[...] = jnp.dot(s, w_ref[...], preferred_element_type=F32) + b_ref[...]


def _adaln(c, w, b):
    bsz, d = c.shape
    n = w.shape[1]
    bn = 1536
    return pl.pallas_call(
        _adaln_kernel,
        grid=(n // bn,),
        in_specs=[pl.BlockSpec((bsz, d), lambda j: (0, 0)),
                  pl.BlockSpec((d, bn), lambda j: (0, j)),
                  pl.BlockSpec((1, bn), lambda j: (0, j))],
        out_specs=pl.BlockSpec((bsz, bn), lambda j: (0, j)),
        out_shape=jax.ShapeDtypeStruct((bsz, n), F32),
        compiler_params=_cparams(("arbitrary",)),
        name="adaln",
    )(c, w, b.reshape(1, n))


def _rms_mod(x, g, shift, scale):
    ms = jnp.mean(x * x, axis=-1, keepdims=True)
    y = x * lax.rsqrt(ms + EPS) * g
    return y * (1.0 + scale) + shift


def _rope(x, cos, sin):
    lane = lax.broadcasted_iota(jnp.int32, x.shape, 1)
    first_half = (lane & (HEAD_DIM - 1)) < (HEAD_DIM // 2)
    swapped = jnp.where(first_half,
                        pltpu.roll(x, LANES - HEAD_DIM // 2, axis=1),
                        pltpu.roll(x, HEAD_DIM // 2, axis=1))
    return x * cos + swapped * sin


def _inproj_kernel(x_ref, mod_ref, g_ref, w_ref, cos_ref, sin_ref,
                   qs_ref, ks_ref, vs_ref, qd_ref, kd_ref, vd_ref, gate_ref):
    x = x_ref[0]
    h = _rms_mod(x, g_ref[...], mod_ref[0, 0:1, :], mod_ref[0, 1:2, :]).astype(BF16)
    scale = HEAD_DIM ** -0.5

    def proj(lo, n):
        return jnp.dot(h, w_ref[:, lo:lo + n], preferred_element_type=F32)

    o = 0
    qs_ref[0] = (proj(o, SB_WIDTH) * scale).astype(BF16); o += SB_WIDTH
    ks_ref[0] = proj(o, SB_WIDTH).astype(BF16); o += SB_WIDTH
    vs_ref[0] = proj(o, SB_WIDTH).astype(BF16); o += SB_WIDTH
    cos = cos_ref[...]
    sin = sin_ref[...]
    for ref, sc in ((qd_ref, scale), (kd_ref, 1.0)):
        for j in range(DIL_WIDTH // LANES):
            p = proj(o + j * LANES, LANES)
            ref[0, :, j * LANES:(j + 1) * LANES] = (_rope(p, cos, sin) * sc).astype(BF16)
        o += DIL_WIDTH
    vd_ref[0] = proj(o, DIL_WIDTH).astype(BF16); o += DIL_WIDTH
    d = x.shape[-1]
    gate_ref[0] = jax.nn.sigmoid(proj(o, 2 * d))


def _inproj(x, mod, g_mix, w_in_bf, cos, sin, tm=256):
    b, s, d = x.shape
    ncols = w_in_bf.shape[1]
    tok = lambda w: pl.BlockSpec((1, tm, w), lambda bi, i: (bi, i, 0))
    outs = [(SB_WIDTH, BF16)] * 3 + [(DIL_WIDTH, BF16)] * 3 + [(2 * d, F32)]
    return pl.pallas_call(
        _inproj_kernel,
        grid=(b, s // tm),
        in_specs=[tok(d),
                  pl.BlockSpec((1, N_MOD, d), lambda bi, i: (bi, 0, 0)),
                  _const_spec((1, d)),
                  _const_spec((d, ncols)),
                  pl.BlockSpec((tm, LANES), lambda bi, i: (i, 0)),
                  pl.BlockSpec((tm, LANES), lambda bi, i: (i, 0))],
        out_specs=[tok(w) for w, _ in outs],
        out_shape=[jax.ShapeDtypeStruct((b, s, w), dt) for w, dt in outs],
        compiler_params=_cparams(("parallel", "parallel")),
        name="inproj",
    )(x, mod, g_mix, w_in_bf, cos, sin)


SB_BLOCK = 256
F32_EXP_UNDERFLOW = -104.0


def _neg_softplus(z):
    return -(jnp.maximum(z, 0.0) + jnp.log(1.0 + jnp.exp(-jnp.abs(z))))


def _sb_kernel(q_ref, k_ref, v_ref, tri_ref, o_ref):
    i = pl.program_id(2)
    blk = SB_BLOCK
    q = q_ref[0]
    lane = lax.broadcasted_iota(jnp.int32, q.shape, 1)
    q2 = jnp.concatenate([jnp.where(lane < HEAD_DIM, q, jnp.zeros_like(q)),
                          jnp.where(lane >= HEAD_DIM, q, jnp.zeros_like(q))], axis=0)
    tri = tri_ref[...]

    def block(kb, carry, causal):
        run, acc = carry
        start = pl.multiple_of(kb * blk, blk)
        k = k_ref[0, pl.ds(start, blk), :]
        v = v_ref[0, pl.ds(start, blk), :]
        z = lax.dot_general(q2, k, (((1,), (1,)), ((), ())), preferred_element_type=F32)
        lsm = _neg_softplus(z)
        if causal:
            row = lax.broadcasted_iota(jnp.int32, z.shape, 0) & (blk - 1)
            col = lax.broadcasted_iota(jnp.int32, z.shape, 1)
            valid = col < row
            lsm = jnp.where(valid, lsm, 0.0)
        hi = lsm.astype(BF16)
        lo = (lsm - hi.astype(F32)).astype(BF16)
        cs = (jnp.dot(hi, tri, preferred_element_type=F32)
              + jnp.dot(lo, tri, preferred_element_type=F32))
        a = jnp.exp(z + cs + run)
        if causal:
            a = jnp.where(valid, a, 0.0)
        acc = acc + jnp.dot(a.astype(BF16), v, preferred_element_type=F32)
        return run + cs[:, 0:1], acc

    def alive(carry):
        top = jnp.max(carry[0], axis=0, keepdims=True)
        return (top[0, 0] >= F32_EXP_UNDERFLOW).astype(jnp.int32)

    first = block(i, (jnp.zeros((2 * blk, 1), F32), jnp.zeros((2 * blk, LANES), F32)), causal=True)

    def cond(state):
        return (state[0] >= 0) & (state[1] > 0)

    def body(state):
        kb, _, carry = state
        carry = block(kb, carry, causal=False)
        return kb - 1, alive(carry), carry

    acc = lax.while_loop(cond, body, (i - 1, alive(first), first))[2][1]
    o_ref[0] = jnp.where(lane < HEAD_DIM, acc[0:blk], acc[blk:2 * blk]).astype(o_ref.dtype)


def _sb_attention(q, k, v):
    b, s, w = q.shape
    blk = SB_BLOCK
    r = lax.broadcasted_iota(jnp.int32, (blk, blk), 0)
    c = lax.broadcasted_iota(jnp.int32, (blk, blk), 1)
    tri = (r >= c).astype(BF16)
    return pl.pallas_call(
        _sb_kernel,
        grid=(b, w // LANES, s // blk),
        in_specs=[pl.BlockSpec((1, blk, LANES), lambda bi, p, i: (bi, i, p)),
                  pl.BlockSpec((1, s, LANES), lambda bi, p, i: (bi, 0, p)),
                  pl.BlockSpec((1, s, LANES), lambda bi, p, i: (bi, 0, p)),
                  _const_spec((blk, blk))],
        out_specs=pl.BlockSpec((1, blk, LANES), lambda bi, p, i: (bi, i, p)),
        out_shape=jax.ShapeDtypeStruct((b, s, w), BF16),
        compiler_params=_cparams(("parallel", "parallel", "arbitrary")),
        name="sb_attn",
    )(q, k, v, tri)


DIL_BLOCK = 128
DIL_SUB = 2


def _dil_kernel(q_ref, kp_ref, kc_ref, vp_ref, vc_ref, o_ref, lse_ref):
    i = pl.program_id(2)
    blk = DIL_BLOCK
    kall = jnp.concatenate([kp_ref[0], kc_ref[0]], axis=0)
    vall = jnp.concatenate([vp_ref[0], vc_ref[0]], axis=0)
    nh = DIL_HEADS_PER_GROUP
    row = lax.broadcasted_iota(jnp.int32, (nh * blk, 2 * blk), 0) & (blk - 1)
    col = lax.broadcasted_iota(jnp.int32, (nh * blk, 2 * blk), 1)
    band = (col >= row) & (col <= row + blk)
    lane = lax.broadcasted_iota(jnp.int32, (blk, q_ref.shape[-1]), 1)
    mine = [(lane >= h * HEAD_DIM) & (lane < (h + 1) * HEAD_DIM) for h in range(nh)]
    for sb in range(DIL_SUB):
        q = q_ref[0, sb * blk:(sb + 1) * blk, :]
        k2 = kall[sb * blk:(sb + 2) * blk]
        v2 = vall[sb * blk:(sb + 2) * blk]
        valid = band & ((col >= blk) | (i > 0)) if sb == 0 else band
        q4 = jnp.concatenate([jnp.where(mine[h], q, jnp.zeros_like(q)) for h in range(nh)], axis=0)
        z = lax.dot_general(q4, k2, (((1,), (1,)), ((), ())), preferred_element_type=F32)
        z = jnp.where(valid, z, NEG_BIG)
        m = jnp.max(z, axis=-1, keepdims=True)
        p = jnp.exp(z - m)
        l = jnp.sum(p, axis=-1, keepdims=True)
        o4 = jnp.dot(p.astype(BF16), v2, preferred_element_type=F32) / l
        lse4 = m + jnp.log(l)
        out = jnp.zeros(q.shape, F32)
        lse_full = jnp.zeros(q.shape, F32)
        for h in range(nh):
            out = jnp.where(mine[h], o4[h * blk:(h + 1) * blk], out)
            lse_full = jnp.where(mine[h], lse4[h * blk:(h + 1) * blk], lse_full)
        o_ref[0, sb * blk:(sb + 1) * blk, :] = out
        lse_ref[0, sb * blk:(sb + 1) * blk, :] = lse_full


def _dil_attention_group(q, k, v, g, r):
    b, s, w = q.shape
    sr = s // r
    blk = DIL_BLOCK
    gw = DIL_OUT_WIDTH
    per = w // gw
    qv, kv, vv = (a.reshape(b, sr, r * w) for a in (q, k, v))
    step = DIL_SUB * blk
    cur = pl.BlockSpec((1, step, gw), lambda bi, c, i: (bi, i, per * c + g))
    prev = pl.BlockSpec((1, blk, gw), lambda bi, c, i: (bi, jnp.maximum(DIL_SUB * i - 1, 0), per * c + g))
    ospec = pl.BlockSpec((1, step, gw), lambda bi, c, i: (bi, i, c))
    o, lse = pl.pallas_call(
        _dil_kernel,
        grid=(b, r, sr // step),
        in_specs=[cur, prev, cur, prev, cur],
        out_specs=[ospec, ospec],
        out_shape=[jax.ShapeDtypeStruct((b, sr, r * gw), F32)] * 2,
        compiler_params=_cparams(("parallel", "parallel", "arbitrary")),
        name=f"dil_attn_r{r}",
    )(qv, kv, kv, vv, vv)
    return o.reshape(b, s, gw), lse.reshape(b, s, gw)


def _merge_kernel(osb_ref, o0_ref, o1_ref, o2_ref, l0_ref, l1_ref, l2_ref, gate_ref, x_ref, mod_ref,
                  wsb_ref, wdil_ref, wout_ref, gffn_ref, wpq_ref, keys_ref,
                  x1_ref, h2_ref, sub_ref):
    d = x_ref.shape[-1]
    l0, l1, l2 = l0_ref[0], l1_ref[0], l2_ref[0]
    m = jnp.maximum(jnp.maximum(l0, l1), l2)
    e0, e1, e2 = jnp.exp(l0 - m), jnp.exp(l1 - m), jnp.exp(l2 - m)
    o_d = (e0 * o0_ref[0] + e1 * o1_ref[0] + e2 * o2_ref[0]) / (e0 + e1 + e2)
    br_sb = jnp.dot(osb_ref[0], wsb_ref[...], preferred_element_type=F32)
    br_d = jnp.dot(o_d.astype(BF16), wdil_ref[...], preferred_element_type=F32)
    merged = gate_ref[0, :, 0:d] * br_sb + gate_ref[0, :, d:2 * d] * br_d
    x1 = x_ref[0] + mod_ref[0, 2:3, :] * jnp.dot(merged.astype(BF16), wout_ref[...],
                                                 preferred_element_type=F32)
    x1_ref[0] = x1
    h2 = _rms_mod(x1, gffn_ref[...], mod_ref[0, 3:4, :], mod_ref[0, 4:5, :])
    h2_ref[0] = h2
    pq = jnp.dot(h2.astype(BF16), wpq_ref[...], preferred_element_type=F32).astype(BF16)
    for hp in range(2 * PEER_HEADS):
        sub_ref[hp] = lax.dot_general(keys_ref[hp], pq[:, hp * PEER_HALF:(hp + 1) * PEER_HALF],
                                      (((1,), (1,)), ((), ())), preferred_element_type=F32)


def _merge(o_sb, o_g, lse_g, gates, x, mod, w_sb_o, w_dil_o, w_out, g_ffn, w_pq, keys, tm=256):
    b, s, d = x.shape
    nt = s // tm
    tok = lambda w: pl.BlockSpec((1, tm, w), lambda bi, i: (bi, i, 0))
    nhp = 2 * PEER_HEADS
    return pl.pallas_call(
        _merge_kernel,
        grid=(b, nt),
        in_specs=[tok(SB_WIDTH)] + [tok(DIL_OUT_WIDTH)] * 6 + [tok(2 * d), tok(d),
                  pl.BlockSpec((1, N_MOD, d), lambda bi, i: (bi, 0, 0)),
                  _const_spec(w_sb_o.shape), _const_spec(w_dil_o.shape), _const_spec(w_out.shape),
                  _const_spec((1, d)), _const_spec(w_pq.shape), _const_spec(keys.shape)],
        out_specs=[tok(d), tok(d),
                   pl.BlockSpec((nhp, PEER_NKEYS, tm), lambda bi, i: (0, 0, bi * nt + i))],
        out_shape=[jax.ShapeDtypeStruct((b, s, d), F32), jax.ShapeDtypeStruct((b, s, d), F32),
                   jax.ShapeDtypeStruct((nhp, PEER_NKEYS, b * s), F32)],
        compiler_params=_cparams(("parallel", "parallel")),
        name="merge",
    )(o_sb, *o_g, *lse_g, gates, x, mod, w_sb_o, w_dil_o, w_out, g_ffn, w_pq, keys)


def _top_rows(s, k, payload=None):
    n = s.shape[0]
    rowid = lax.broadcasted_iota(jnp.int32, s.shape, 0).astype(F32)
    vals, sel = [], []
    for _ in range(k):
        m = jnp.max(s, axis=0, keepdims=True)
        ix = jnp.min(jnp.where(s == m, rowid, float(n)), axis=0, keepdims=True)
        hit = rowid == ix
        vals.append(m)
        sel.append(ix if payload is None else jnp.max(jnp.where(hit, payload, -1.0), axis=0, keepdims=True))
        s = jnp.where(hit, -jnp.inf, s)
    return jnp.concatenate(vals, axis=0), jnp.concatenate(sel, axis=0)


def _topk_kernel(sub_ref, ids_ref, gate_ref):
    k = PEER_TOPK
    tl = sub_ref.shape[-1]
    half = k // 2
    jrow = lax.broadcasted_iota(jnp.int32, (half, tl), 0)
    experts, gates = [], []
    for h in range(PEER_HEADS):
        s0, i0 = _top_rows(sub_ref[2 * h], k)
        s1, i1 = _top_rows(sub_ref[2 * h + 1], k)
        cand = [s0[0:1, :] + s1]
        cand_e = [i0[0:1, :] * PEER_NKEYS + i1]
        for i in range(1, half):
            keep = jrow < (k // (i + 1))
            cand.append(jnp.where(keep, s0[i:i + 1, :] + s1[0:half, :], -jnp.inf))
            cand_e.append(i0[i:i + 1, :] * PEER_NKEYS + i1[0:half, :])
        cand.append(s0[half:k, :] + s1[0:1, :])
        cand_e.append(i0[half:k, :] * PEER_NKEYS + i1[0:1, :])
        best, expert = _top_rows(jnp.concatenate(cand, axis=0), k, jnp.concatenate(cand_e, axis=0))
        e = jnp.exp(best - best[0:1, :])
        experts.append(expert)
        gates.append(e / jnp.sum(e, axis=0, keepdims=True))
    ids_ref[...] = jnp.concatenate(experts, axis=0).T.astype(jnp.int32)
    gate_ref[...] = jnp.concatenate(gates, axis=0).T


def _peer_topk(sub_t, tl=256):
    nhp, nk, t = sub_t.shape
    hk = PEER_HEADS * PEER_TOPK
    return pl.pallas_call(
        _topk_kernel,
        grid=(t // tl,),
        in_specs=[pl.BlockSpec((nhp, nk, tl), lambda i: (0, 0, i))],
        out_specs=[pl.BlockSpec((tl, hk), lambda i: (i, 0))] * 2,
        out_shape=[jax.ShapeDtypeStruct((t, hk), jnp.int32), jax.ShapeDtypeStruct((t, hk), F32)],
        compiler_params=_cparams(("parallel",)),
        name="peer_topk",
    )(sub_t)


PEER_TT = 8
PEER_BUFS = 4
PEER_AHEAD = 2
ROW_TILES = 8
PAIR_ROWS = 2 * ROW_TILES
DMA_THREADS = 2
ISSUE_UNROLL = 16


def _peer_issue(ids_ref, uv_ref, buf, sem):
    for n in range(ids_ref.shape[-1]):
        dst = buf.at[n * PAIR_ROWS:(n + 1) * PAIR_ROWS, :]
        pltpu.make_async_copy(uv_ref.at[ids_ref[0, 0, n]], dst, sem).start(priority=n % DMA_THREADS)


def _peer_issue_rolled(ids_ref, uv_ref, buf, sem):
    def chunk(j, _):
        for kk in range(ISSUE_UNROLL):
            n = j * ISSUE_UNROLL + kk
            dst = buf.at[pl.ds(pl.multiple_of(n * PAIR_ROWS, PAIR_ROWS), PAIR_ROWS), :]
            pltpu.make_async_copy(uv_ref.at[ids_ref[0, 0, n]], dst, sem).start(priority=kk % DMA_THREADS)
        return 0

    lax.fori_loop(0, ids_ref.shape[-1] // ISSUE_UNROLL, chunk, 0)


def _peer_wait(buf, sem):
    pltpu.make_async_copy(buf, buf, sem).wait()


def _peer_compute(gate, x_ref, t0, gsum_ref, gexp_ref, buf, o_ref):
    hk = gate.shape[-1]
    rows = hk * PAIR_ROWS
    lane = lax.broadcasted_iota(jnp.int32, (ROW_TILES, rows), 1)
    sub = lax.broadcasted_iota(jnp.int32, (ROW_TILES, rows), 0)
    tile_row = lane & (PAIR_ROWS - 1)
    diag_u = tile_row == sub
    diag_v = tile_row == sub + ROW_TILES
    zs = []
    for t in range(PEER_TT):
        xs = x_ref[t0 + t].astype(BF16)
        uvt = buf[t * rows:(t + 1) * rows, :]
        y = lax.dot_general(xs, uvt, (((1,), (1,)), ((), ())), preferred_element_type=F32)
        zs.append(jnp.sum(jnp.where(diag_u, y, 0.0), axis=0, keepdims=True))
    z = jnp.concatenate(zs, axis=0)
    zh = z.astype(BF16)
    zl = (z - zh.astype(F32)).astype(BF16)
    act = (jnp.dot(zh, gsum_ref[...], preferred_element_type=F32)
           + jnp.dot(zl, gsum_ref[...], preferred_element_type=F32))
    c = (gate * jax.nn.gelu(act)).astype(BF16)
    crep = jnp.dot(c, gexp_ref[...], preferred_element_type=F32)
    for t in range(PEER_TT):
        cm = jnp.where(diag_v, jnp.broadcast_to(crep[t:t + 1, :], diag_v.shape), 0.0).astype(BF16)
        o_ref[t0 + t] = jnp.dot(cm, buf[t * rows:(t + 1) * rows, :], preferred_element_type=F32)


def _peer_kernel(*refs):
    nb = PEER_BUFS
    ids_now = refs[:nb]
    ids_nxt = refs[nb:nb + PEER_AHEAD]
    gate_ref, x_ref, gsum_ref, gexp_ref, uv_ref, o_ref = refs[nb + PEER_AHEAD:nb + PEER_AHEAD + 6]
    bufs = refs[nb + PEER_AHEAD + 6:2 * nb + PEER_AHEAD + 6]
    sems = refs[2 * nb + PEER_AHEAD + 6:]
    i = pl.program_id(0)

    @pl.when(i == 0)
    def _():
        for p in range(PEER_AHEAD):
            _peer_issue_rolled(ids_now[p], uv_ref, bufs[p], sems[p])

    for p in range(nb):
        _peer_wait(bufs[p], sems[p])
        q = p + PEER_AHEAD
        ids_q = ids_now[q] if q < nb else ids_nxt[q - nb]
        _peer_issue(ids_q, uv_ref, bufs[q % nb], sems[q % nb])
        _peer_compute(gate_ref[p * PEER_TT:(p + 1) * PEER_TT, :], x_ref, p * PEER_TT, gsum_ref, gexp_ref,
                      bufs[p], o_ref)

    @pl.when(i == pl.num_programs(0) - 1)
    def _():
        for p in range(PEER_AHEAD):
            _peer_wait(bufs[p], sems[p])


def _peer_ffn(ids, gate, h2_tiles, uv):
    t, hk = ids.shape
    rows = hk * PAIR_ROWS
    r = lax.broadcasted_iota(jnp.int32, (rows, hk), 0)
    c = lax.broadcasted_iota(jnp.int32, (rows, hk), 1)
    gsum = (r // PAIR_ROWS == c).astype(BF16)
    gexp = gsum.T
    tt, nb = PEER_TT, PEER_BUFS
    nblk = t // tt
    nstep = nblk // nb
    ids3 = ids.reshape(nblk, 1, tt * hk)
    smem = lambda f: pl.BlockSpec((1, 1, tt * hk), f, memory_space=pltpu.SMEM)
    now = [smem(functools.partial(lambda i, p: (nb * i + p, 0, 0), p=p)) for p in range(nb)]
    nxt = [smem(functools.partial(lambda i, p: (nb * jnp.minimum(i + 1, nstep - 1) + p, 0, 0), p=p))
           for p in range(PEER_AHEAD)]
    return pl.pallas_call(
        _peer_kernel,
        grid=(nstep,),
        in_specs=now + nxt + [
            pl.BlockSpec((nb * tt, hk), lambda i: (i, 0)),
            pl.BlockSpec((nb * tt, ROW_TILES, LANES), lambda i: (i, 0, 0)),
            _const_spec((rows, hk)), _const_spec((hk, rows)),
            pl.BlockSpec(memory_space=pl.ANY)],
        out_specs=pl.BlockSpec((nb * tt, ROW_TILES, LANES), lambda i: (i, 0, 0)),
        out_shape=jax.ShapeDtypeStruct((t, ROW_TILES, LANES), F32),
        scratch_shapes=[pltpu.VMEM((tt * rows, LANES), BF16)] * nb + [pltpu.SemaphoreType.DMA(())] * nb,
        compiler_params=_cparams(("arbitrary",)),
        name="peer_ffn",
    )(*([ids3] * (nb + PEER_AHEAD)), gate, h2_tiles, gsum, gexp, uv)


def _final_kernel(x1_ref, f_ref, mod_ref, g_ref, o_ref):
    x2 = x1_ref[0] + mod_ref[0, 5:6, :] * f_ref[0]
    ms = jnp.mean(x2 * x2, axis=-1, keepdims=True)
    o_ref[0] = x2 * lax.rsqrt(ms + EPS) * g_ref[...]


def _final(x1, ffn, mod, g_final, tm=512):
    b, s, d = x1.shape
    tok = pl.BlockSpec((1, tm, d), lambda bi, i: (bi, i, 0))
    return pl.pallas_call(
        _final_kernel,
        grid=(b, s // tm),
        in_specs=[tok, tok, pl.BlockSpec((1, N_MOD, d), lambda bi, i: (bi, 0, 0)), _const_spec((1, d))],
        out_specs=tok,
        out_shape=jax.ShapeDtypeStruct((b, s, d), F32),
        compiler_params=_cparams(("parallel", "parallel")),
        name="final",
    )(x1, ffn, mod, g_final)


def _rope_tables(s):
    half = HEAD_DIM // 2
    inv = ROPE_THETA ** (-jnp.arange(half, dtype=F32) / half)
    ang = jnp.arange(s, dtype=F32)[:, None] * inv[None, :]
    cos, sin = jnp.cos(ang), jnp.sin(ang)
    reps = LANES // HEAD_DIM
    return (jnp.tile(jnp.concatenate([cos, cos], axis=1), (1, reps)),
            jnp.tile(jnp.concatenate([-sin, sin], axis=1), (1, reps)))


def kernel(x, c, w_ada, b_ada, g_mix, w_in, w_sb_o, w_dil_o, w_out, g_ffn, w_pq, peer_keys, peer_u, peer_v,
           g_final):
    b, s, d = x.shape
    assert w_ada.shape[0] == 1, "the final residual and norm are fused: single layer only"
    layer = 0
    cos, sin = _rope_tables(s)
    mod = _adaln(c, w_ada[layer], b_ada[layer]).reshape(b, N_MOD, d)
    q_sb, k_sb, v_sb, q_d, k_d, v_d, gates = _inproj(
        x, mod, g_mix[layer].reshape(1, d), w_in[layer].astype(BF16), cos, sin)
    o_sb = _sb_attention(q_sb, k_sb, v_sb)
    o_g, lse_g = zip(*[_dil_attention_group(q_d, k_d, v_d, g, r) for g, (_, r) in enumerate(DIL_GROUPS)])
    keys = peer_keys[layer].reshape(2 * PEER_HEADS, PEER_NKEYS, PEER_HALF).astype(BF16)
    x1, h2, sub_t = _merge(o_sb, o_g, lse_g, gates, x, mod,
                           w_sb_o[layer].astype(BF16), w_dil_o[layer].astype(BF16),
                           w_out[layer].astype(BF16), g_ffn[layer].reshape(1, d),
                           w_pq[layer].astype(BF16), keys)
    ids_t, gate_t = _peer_topk(sub_t)
    n_exp = peer_u.shape[1]
    uv = jnp.concatenate([peer_u[layer].reshape(n_exp, ROW_TILES, LANES),
                          peer_v[layer].reshape(n_exp, ROW_TILES, LANES)], axis=1).astype(BF16)
    ffn = _peer_ffn(ids_t, gate_t, h2.reshape(b * s, ROW_TILES, LANES), uv)
    return _final(x1, ffn.reshape(b, s, d), mod, g_final.reshape(1, d))
```
